```python
import jax, jax.numpy as jnp
from jax import lax
import numpy as np

D_MODEL = 1024
BATCH = 16
SEQ = 2048
DEPTH = 1

GRID_W = 64
CTX_LEN = 256
N_HEADS_M = 4
HEAD_DIM_M = 128
W_M = N_HEADS_M * HEAD_DIM_M
N_HEADS_H = 4
HEAD_DIM_H = 128
W_H = N_HEADS_H * HEAD_DIM_H
CONV_K = 3
CHUNK_M = 64
CHUNK_H = 32
N_EXPERTS = 32
TOP_K = 4
D_EXPERT = D_MODEL
SWIGLU_LIMIT = 7.0
SWIGLU_ALPHA = 1.702
MOE_BLOCK = 256
EPS = 1e-6
SPLIT_SIZES = (2 * W_M, W_M, W_M, W_H, W_H, W_H, 2 * W_H, D_MODEL, D_MODEL, 4 * N_HEADS_M)
SPLIT_POINTS = tuple(np.cumsum(SPLIT_SIZES)[:-1].tolist())
N_IN = int(sum(SPLIT_SIZES))

kernel_name = 'hybrid_mlstm_hgrn2_moe_dit'


def rmsnorm(x, w):
    xf = x.astype(jnp.float32)
    y = xf * lax.rsqrt(jnp.mean(xf * xf, axis=-1, keepdims=True) + EPS)
    return (y * w.astype(jnp.float32)).astype(x.dtype)


def head_norm(o, w, dtype):
    B, H, T, d = o.shape
    y = o * lax.rsqrt(jnp.mean(o * o, axis=-1, keepdims=True) + EPS) * w.astype(jnp.float32).reshape(H, 1, d)
    return y.transpose(0, 2, 1, 3).reshape(B, T, H * d).astype(dtype)


def ada_mod(cvec, w, b):
    return jnp.split(jax.nn.silu(cvec) @ w + b, 6, axis=-1)


def modulate(x, shift, scale):
    return x * (1 + scale[:, None, :]) + shift[:, None, :]


def split_heads(a, H):
    B, T, W = a.shape
    return a.reshape(B, T, H, W // H).transpose(0, 2, 1, 3)


def short_conv(a, w, rows, cols):
    B, T, C = a.shape
    y = lax.conv_general_dilated(a.reshape(B, rows, cols, C), w[:, :, None, :].astype(a.dtype), (1, 1), 'SAME',
                                 dimension_numbers=('NHWC', 'HWIO', 'NHWC'), feature_group_count=C)
    return y.reshape(B, T, C)


def to_chunks(a, L):
    B, H, T = a.shape[:3]
    return jnp.moveaxis(a.reshape(B, H, T // L, L, *a.shape[3:]), 2, 0)


def from_chunks(a):
    NC, B, H, L = a.shape[:4]
    return jnp.moveaxis(a, 0, 2).reshape(B, H, NC * L, *a.shape[4:])


def mlstm_scan(q, k, v, ig, lf, state0, with_output):
    L = CHUNK_M
    mask = jnp.tril(jnp.ones((L, L), bool))
    xs = tuple(to_chunks(a, L) for a in (q, k, v, ig, lf))

    def step(carry, inp):
        C, n, m = carry
        qc, kc, vc, ic, fc = inp
        b = jnp.cumsum(fc, axis=-1)
        bL = b[..., -1]
        logw = bL[..., None] - b + ic
        m_new = jnp.maximum(bL + m, jnp.max(logw, axis=-1))
        w = jnp.exp(logw - m_new[..., None])
        decay = jnp.exp(bL + m - m_new)
        C_new = decay[..., None, None] * C + jnp.einsum('bhs,bhsv,bhsd->bhvd', w, vc, kc)
        n_new = decay[..., None] * n + jnp.einsum('bhs,bhsd->bhd', w, kc)
        if not with_output:
            return (C_new, n_new, m_new), None
        logD = jnp.where(mask, b[..., :, None] - b[..., None, :] + ic[..., None, :], -jnp.inf)
        m_inter = b + m[..., None]
        m_t = jnp.maximum(jnp.max(logD, axis=-1), m_inter)
        s = jnp.einsum('bhtd,bhsd->bhts', qc, kc) * jnp.exp(logD - m_t[..., None])
        inter = jnp.exp(m_inter - m_t)
        num = jnp.einsum('bhts,bhsv->bhtv', s, vc) + inter[..., None] * jnp.einsum('bhtd,bhvd->bhtv', qc, C)
        den = jnp.sum(s, axis=-1) + inter * jnp.einsum('bhtd,bhd->bht', qc, n)
        h = num / jnp.maximum(jnp.abs(den), jnp.exp(-m_t))[..., None]
        return (C_new, n_new, m_new), h

    state, h = lax.scan(step, state0, xs)
    return (from_chunks(h) if with_output else None), state


def hgrn_scan(q, k, i, lf, S0, with_output):
    L = CHUNK_H
    mask = jnp.tril(jnp.ones((L, L), bool))
    xs = tuple(to_chunks(a, L) for a in (q, k, i, lf))

    def step(S, inp):
        qc, kc, ic, fc = inp
        A = jnp.cumsum(fc, axis=2)
        AL = A[:, :, -1]
        S_new = jnp.exp(AL)[..., None] * S + jnp.einsum('bhsd,bhsv->bhdv', kc * jnp.exp(AL[:, :, None] - A), ic)
        if not with_output:
            return S_new, None
        diff = jnp.where(mask[:, :, None], A[:, :, :, None, :] - A[:, :, None, :, :], -jnp.inf)
        s = jnp.einsum('bhtsd,bhsd->bhts', qc[:, :, :, None, :] * jnp.exp(diff), kc)
        o = jnp.einsum('bhts,bhsv->bhtv', s, ic) + jnp.einsum('bhtd,bhdv->bhtv', qc * jnp.exp(A), S)
        return S_new, o

    S, o = lax.scan(step, S0, xs)
    return (from_chunks(o) if with_output else None), S


def flip_t(a, d):
    return jnp.flip(a, axis=2) if d else a


def bidir(scan_fn, lat_dirs, ctx_dirs, state0, ctx_out):
    outs_l, outs_c = [], []
    for d in range(2):
        hc, st = scan_fn(*[flip_t(a, d) for a in ctx_dirs[d]], state0, ctx_out)
        hl, _ = scan_fn(*[flip_t(a, d) for a in lat_dirs[d]], st, True)
        outs_l.append(flip_t(hl, d))
        if ctx_out:
            outs_c.append(flip_t(hc, d))
    return outs_l[0] + outs_l[1], (outs_c[0] + outs_c[1] if ctx_out else None)


def mixer_features(h, w_in, b_in, conv_w, lb, rows, cols):
    B, T, _ = h.shape
    f32 = jnp.float32
    p = h @ w_in + b_in
    mqk, mv, mo, hq, hi, hg, hf, ga, gb, mg = jnp.split(p, SPLIT_POINTS, axis=-1)
    mq, mk = jnp.split(jax.nn.silu(short_conv(mqk, conv_w, rows, cols)), 2, axis=-1)
    mq = split_heads(mq, N_HEADS_M).astype(f32) * HEAD_DIM_M ** -0.5
    mk = split_heads(mk, N_HEADS_M).astype(f32)
    mv = split_heads(mv, N_HEADS_M).astype(f32)
    mg = mg.astype(f32).reshape(B, T, 4, N_HEADS_M).transpose(2, 0, 3, 1)
    m_dirs = [(mq, mk, mv, mg[d], jax.nn.log_sigmoid(mg[2 + d])) for d in range(2)]
    hq = split_heads(jax.nn.silu(hq), N_HEADS_H).astype(f32)
    hi = split_heads(hi, N_HEADS_H).astype(f32)
    z = hf.astype(f32).reshape(B, T, 2, W_H)
    h_dirs = []
    for d in range(2):
        lb_d, zd = lb[d], z[:, :, d]
        logf = jnp.logaddexp(jnp.log(lb_d), jnp.log1p(-lb_d) + jax.nn.log_sigmoid(zd))
        kk = (1 - lb_d) * jax.nn.sigmoid(-zd)
        h_dirs.append((hq, split_heads(kk, N_HEADS_H), hi, split_heads(logf, N_HEADS_H)))
    return m_dirs, h_dirs, (mo, hg, ga, gb)


def merge_branches(mo_h, hg_h, gates, m_norm, h_norm, w_pa, w_pb, w_out, dtype):
    mo, hg, ga, gb = gates
    a = head_norm(mo_h, m_norm, dtype) * jax.nn.sigmoid(mo)
    b = head_norm(hg_h, h_norm, dtype) * jax.nn.silu(hg)
    y = jax.nn.sigmoid(ga) * (a @ w_pa) + jax.nn.sigmoid(gb) * (b @ w_pb)
    return y @ w_out


def token_mixer(hl, hc, w_in, b_in, conv_w, lb, m_norm, h_norm, w_pa, w_pb, w_out, ctx_out):
    B, S, _ = hl.shape
    rows = S // GRID_W
    lm, lh, lg = mixer_features(hl, w_in, b_in, conv_w, lb, rows, GRID_W)
    cm, ch, cg = mixer_features(hc, w_in, b_in, conv_w, lb, 1, hc.shape[1])
    f32 = jnp.float32
    m_state0 = (jnp.zeros((B, N_HEADS_M, HEAD_DIM_M, HEAD_DIM_M), f32),
                jnp.zeros((B, N_HEADS_M, HEAD_DIM_M), f32), jnp.zeros((B, N_HEADS_M), f32))
    h_state0 = jnp.zeros((B, N_HEADS_H, HEAD_DIM_H, HEAD_DIM_H), f32)
    ml, mc = bidir(mlstm_scan, lm, cm, m_state0, ctx_out)
    gl, gc = bidir(hgrn_scan, lh, ch, h_state0, ctx_out)
    yl = merge_branches(ml, gl, lg, m_norm, h_norm, w_pa, w_pb, w_out, hl.dtype)
    yc = merge_branches(mc, gc, cg, m_norm, h_norm, w_pa, w_pb, w_out, hc.dtype) if ctx_out else None
    return yl, yc


def moe_ffn(h, w_router, b_router, w_gu, b_gu, w_dn, b_dn):
    B, T, D = h.shape
    f32 = jnp.float32
    xf = h.reshape(B * T, D)
    N = B * T
    logits = xf.astype(f32) @ w_router.astype(f32) + b_router.astype(f32)
    top_v, top_e = lax.top_k(logits, TOP_K)
    wts = jax.nn.softmax(top_v, axis=-1)
    e_flat = top_e.reshape(-1)
    tok_flat = jnp.repeat(jnp.arange(N, dtype=jnp.int32), TOP_K)
    order = jnp.argsort(e_flat)
    se, stok, sw = e_flat[order], tok_flat[order], wts.reshape(-1)[order]
    counts = jnp.bincount(e_flat, length=N_EXPERTS)
    pcounts = (counts + MOE_BLOCK - 1) // MOE_BLOCK * MOE_BLOCK
    start = jnp.cumsum(counts) - counts
    pend = jnp.cumsum(pcounts)
    pstart = pend - pcounts
    dest = pstart[se] + jnp.arange(N * TOP_K) - start[se]
    P = -(-N * TOP_K // MOE_BLOCK) * MOE_BLOCK + N_EXPERTS * MOE_BLOCK
    n_blocks = P // MOE_BLOCK
    ptok = jnp.full((P,), N, jnp.int32).at[dest].set(stok)
    pw = jnp.zeros((P,), f32).at[dest].set(sw)
    block_e = jnp.minimum(jnp.searchsorted(pend, jnp.arange(n_blocks) * MOE_BLOCK, side='right'), N_EXPERTS - 1)
    xb = jnp.concatenate([xf, jnp.zeros((1, D), xf.dtype)])[ptok].reshape(n_blocks, MOE_BLOCK, D)

    def expert_block(args):
        xblk, e = args
        gate, up = jnp.split(xblk @ w_gu[e] + b_gu[e], 2, axis=-1)
        gate = jnp.minimum(gate, SWIGLU_LIMIT)
        up = jnp.clip(up, -SWIGLU_LIMIT, SWIGLU_LIMIT)
        act = (up + 1) * gate * jax.nn.sigmoid(SWIGLU_ALPHA * gate)
        return act @ w_dn[e] + b_dn[e]

    yb = lax.map(expert_block, (xb, block_e)).reshape(P, D)
    y = jax.ops.segment_sum(yb * pw[:, None].astype(yb.dtype), ptok, num_segments=N + 1)[:N]
    return y.reshape(B, T, D)


def setup_inputs(seed: int = 0) -> dict:
    key = jax.random.key(seed)
    ks = jax.random.split(key, 32)
    nrm = jax.random.normal
    D, F, E, H = D_MODEL, D_EXPERT, N_EXPERTS, N_HEADS_M
    i_bias = 0.1 * nrm(ks[10], (DEPTH, 2, H))
    f_bias = 3.0 + 3.0 * jax.random.uniform(ks[11], (DEPTH, 2, H))
    b_in = jnp.concatenate([0.02 * nrm(ks[12], (DEPTH, N_IN - 4 * H)),
                            jnp.concatenate([i_bias, f_bias], axis=1).reshape(DEPTH, 4 * H)], axis=-1)
    return {
        'x': nrm(ks[0], (BATCH, SEQ, D)),
        'c': nrm(ks[1], (BATCH, D)),
        'ctx': nrm(ks[2], (BATCH, CTX_LEN, D)),
        'c_ctx': nrm(ks[3], (D,)),
        'w_ada': 0.5 * D ** -0.5 * nrm(ks[4], (DEPTH, D, 6 * D)),
        'b_ada': 0.02 * nrm(ks[5], (DEPTH, 6 * D)),
        'norm_mix_pre': 1.0 + 0.05 * nrm(ks[6], (DEPTH, D)),
        'norm_mix_post': 1.0 + 0.05 * nrm(ks[7], (DEPTH, D)),
        'norm_ffn_pre': 1.0 + 0.05 * nrm(ks[8], (DEPTH, D)),
        'norm_ffn_post': 1.0 + 0.05 * nrm(ks[9], (DEPTH, D)),
        'w_in': D ** -0.5 * nrm(ks[13], (DEPTH, D, N_IN)),
        'b_in': b_in,
        'conv_w': (CONV_K * CONV_K) ** -0.5 * nrm(ks[14], (DEPTH, CONV_K, CONV_K, 2 * W_M)),
        'lb_raw': 0.1 * nrm(ks[15], (DEPTH + 1, 2, W_H)),
        'm_norm': 1.0 + 0.05 * nrm(ks[16], (DEPTH, W_M)),
        'h_norm': 1.0 + 0.05 * nrm(ks[17], (DEPTH, W_H)),
        'w_pa': W_M ** -0.5 * nrm(ks[18], (DEPTH, W_M, D)),
        'w_pb': W_H ** -0.5 * nrm(ks[19], (DEPTH, W_H, D)),
        'w_out': D ** -0.5 * nrm(ks[20], (DEPTH, D, D)),
        'w_router': D ** -0.5 * nrm(ks[21], (DEPTH, D, E)),
        'b_router': 0.01 * nrm(ks[22], (DEPTH, E)),
        'w_gu': D ** -0.5 * nrm(ks[23], (DEPTH, E, D, 2 * F)),
        'b_gu': 0.02 * nrm(ks[24], (DEPTH, E, 2 * F)),
        'w_dn': F ** -0.5 * nrm(ks[25], (DEPTH, E, F, D)),
        'b_dn': 0.02 * nrm(ks[26], (DEPTH, E, D)),
    }


def reference(x, c, ctx, c_ctx, w_ada, b_ada, norm_mix_pre, norm_mix_post, norm_ffn_pre, norm_ffn_post,
              w_in, b_in, conv_w, lb_raw, m_norm, h_norm, w_pa, w_pb, w_out,
              w_router, b_router, w_gu, b_gu, w_dn, b_dn):
    lb_all = jnp.cumsum(jax.nn.softmax(lb_raw.astype(jnp.float32), axis=0), axis=0)
    for l in range(DEPTH):
        last = l == DEPTH - 1
        sh1, sc1, g1, sh2, sc2, g2 = ada_mod(c, w_ada[l], b_ada[l])
        csh1, csc1, cg1, csh2, csc2, cg2 = ada_mod(c_ctx[None, :], w_ada[l], b_ada[l])
        hl = modulate(rmsnorm(x, norm_mix_pre[l]), sh1, sc1)
        hc = modulate(rmsnorm(ctx, norm_mix_pre[l]), csh1, csc1)
        yl, yc = token_mixer(hl, hc, w_in[l], b_in[l], conv_w[l], lb_all[l], m_norm[l], h_norm[l],
                             w_pa[l], w_pb[l], w_out[l], not last)
        x = x + g1[:, None, :] * rmsnorm(yl, norm_mix_post[l])
        hl = modulate(rmsnorm(x, norm_ffn_pre[l]), sh2, sc2)
        x = x + g2[:, None, :] * rmsnorm(moe_ffn(hl, w_router[l], b_router[l], w_gu[l], b_gu[l], w_dn[l], b_dn[l]),
                                         norm_ffn_post[l])
        if not last:
            ctx = ctx + cg1[:, None, :] * rmsnorm(yc, norm_mix_post[l])
            hc = modulate(rmsnorm(ctx, norm_ffn_pre[l]), csh2, csc2)
            ctx = ctx + cg2[:, None, :] * rmsnorm(
                moe_ffn(hc, w_router[l], b_router[l], w_gu[l], b_gu[l], w_dn[l], b_dn[l]), norm_ffn_post[l])
    return x
```

```python
import functools

import jax
import jax.numpy as jnp
from jax import lax
from jax.experimental import pallas as pl
from jax.experimental.pallas import tpu as pltpu

F32 = jnp.float32
BF16 = jnp.bfloat16
I32 = jnp.int32

EPS = 1e-6
N_HEADS = 4
HEAD_DIM = 128
W_MIX = N_HEADS * HEAD_DIM
GRID_W = 64
N_EXPERTS = 32
TOP_K = 4
SWIGLU_LIMIT = 7.0
SWIGLU_ALPHA = 1.702
LANES = 128
VMEM_LIMIT = 56 * 1024 * 1024

MLSTM_CHUNK = 128
HGRN_CHUNK = 64
HGRN_SUB = 16
MOE_BLOCK = 256
NEG_BIG = -1e30
P16_V, P16_MO, P16_HQ, P16_HI, P16_HG = 6, 7, 8, 9, 10


def _cparams(*sem):
    return pltpu.CompilerParams(dimension_semantics=sem, vmem_limit_bytes=VMEM_LIMIT)


def _dot(a, b):
    return jnp.dot(a, b, preferred_element_type=F32)


def _dot_nt(a, b):
    return lax.dot_general(a, b, (((1,), (1,)), ((), ())), preferred_element_type=F32)


def _dot_tn(a, b):
    return lax.dot_general(a, b, (((0,), (0,)), ((), ())), preferred_element_type=F32)


def _split3(x):
    x1 = x.astype(BF16)
    r1 = x - x1.astype(F32)
    x2 = r1.astype(BF16)
    r2 = r1 - x2.astype(F32)
    x3 = r2.astype(BF16)
    return x1, x2, x3


def _mask_dot(mask_bf16, x):
    x1, x2, x3 = _split3(x)
    return _dot(mask_bf16, x1) + _dot(mask_bf16, x2) + _dot(mask_bf16, x3)


def _dot_mask(x, mask_bf16):
    x1, x2, x3 = _split3(x)
    return _dot(x1, mask_bf16) + _dot(x2, mask_bf16) + _dot(x3, mask_bf16)


def _dot_f32(a, b):
    a1, a2, a3 = _split3(a)
    b1, b2, b3 = _split3(b)
    return (_dot(a1, b1) + (_dot(a1, b2) + _dot(a2, b1))
            + (_dot(a2, b2) + _dot(a1, b3) + _dot(a3, b1)))


def _sigmoid(x):
    return 1.0 / (1.0 + jnp.exp(-x))


def _log_sigmoid(x):
    return jnp.minimum(x, 0.0) - jnp.log(1.0 + jnp.exp(-jnp.abs(x)))


def _rms(xf, w):
    return xf * lax.rsqrt(jnp.mean(xf * xf, axis=-1, keepdims=True) + EPS) * w


def _iota(shape, dim):
    return lax.broadcasted_iota(I32, shape, dim)


def _ada_kernel(c_ref, w_ref, b_ref, o_ref):
    cv = c_ref[...]
    s = cv * _sigmoid(cv)
    o_ref[...] = _dot_f32(s, w_ref[...]) + b_ref[...]


def _ada_mod(c_all, w, b):
    rows, d = c_all.shape
    n = w.shape[1]
    tn = 1536
    return pl.pallas_call(
        _ada_kernel,
        grid=(n // tn,),
        in_specs=[pl.BlockSpec((rows, d), lambda j: (0, 0)),
                  pl.BlockSpec((d, tn), lambda j: (0, j)),
                  pl.BlockSpec((1, tn), lambda j: (0, j))],
        out_specs=pl.BlockSpec((rows, tn), lambda j: (0, j)),
        out_shape=jax.ShapeDtypeStruct((rows, n), F32),
        compiler_params=_cparams("parallel"),
        name="ada_mod",
    )(c_all, w, b.reshape(1, n))


def _proj_kernel(x_ref, nw_ref, sc_ref, sh_ref, w_ref, b_ref, o_ref, h_scr):
    @pl.when(pl.program_id(1) == 0)
    def _():
        h = _rms(x_ref[...], nw_ref[...]) * (1.0 + sc_ref[0]) + sh_ref[0]
        h_scr[...] = h.astype(BF16)

    o_ref[...] = (_dot(h_scr[...], w_ref[...]) + b_ref[...]).astype(o_ref.dtype)


def _proj(x2d, nw, sc, sh, w, b, tokens_per_mod, out_dtype, tn):
    m, d = x2d.shape
    n = w.shape[1]
    tm = min(1024, tokens_per_mod)
    per = tokens_per_mod // tm
    return pl.pallas_call(
        _proj_kernel,
        grid=(m // tm, n // tn),
        in_specs=[pl.BlockSpec((tm, d), lambda i, j: (i, 0)),
                  pl.BlockSpec((1, d), lambda i, j: (0, 0)),
                  pl.BlockSpec((1, 1, d), lambda i, j: (i // per, 0, 0)),
                  pl.BlockSpec((1, 1, d), lambda i, j: (i // per, 0, 0)),
                  pl.BlockSpec((d, tn), lambda i, j: (0, j)),
                  pl.BlockSpec((1, tn), lambda i, j: (0, j))],
        out_specs=pl.BlockSpec((tm, tn), lambda i, j: (i, j)),
        out_shape=jax.ShapeDtypeStruct((m, n), out_dtype),
        scratch_shapes=[pltpu.VMEM((tm, d), BF16)],
        compiler_params=_cparams("parallel", "arbitrary"),
        name="in_proj",
    )(x2d, nw, sc, sh, w, b)


def _conv_kernel(x_ref, w_ref, o_ref, *, cols, q_blocks):
    t, c = x_ref.shape
    x = x_ref[...].astype(F32)
    col = lax.rem(_iota((t, c), 0), cols)
    xm = jnp.where(col == 0, 0.0, pltpu.roll(x, 1, 0))
    xp = jnp.where(col == cols - 1, 0.0, pltpu.roll(x, t - 1, 0))

    def hrow(dr):
        return w_ref[3 * dr:3 * dr + 1, :] * xm + w_ref[3 * dr + 1:3 * dr + 2, :] * x + w_ref[3 * dr + 2:3 * dr + 3, :] * xp

    y = hrow(1)
    if t > cols:
        z = jnp.zeros((cols, c), F32)
        y = y + jnp.concatenate([z, hrow(0)[:t - cols]], axis=0) + jnp.concatenate([hrow(2)[cols:], z], axis=0)
    y = y * _sigmoid(y)
    scale = jnp.where(pl.program_id(1) < q_blocks, HEAD_DIM ** -0.5, 1.0)
    o_ref[...] = (y * scale).astype(o_ref.dtype)


def _conv_silu(p16, conv_w9, batch, t, cols):
    cw = 256
    nblk = 2 * W_MIX // cw
    return pl.pallas_call(
        functools.partial(_conv_kernel, cols=cols, q_blocks=W_MIX // cw),
        grid=(batch, nblk),
        in_specs=[pl.BlockSpec((t, cw), lambda b, j: (b, j)),
                  pl.BlockSpec((16, cw), lambda b, j: (0, j))],
        out_specs=pl.BlockSpec((t, cw), lambda b, j: (b, j)),
        out_shape=jax.ShapeDtypeStruct((batch * t, 2 * W_MIX), BF16),
        compiler_params=_cparams("parallel", "parallel"),
        name="conv_silu",
    )(p16, conv_w9)


def _mlstm_gate_sums(gcol, grow, rev):
    L = gcol.shape[0]
    r = _iota((L, L), 0)
    c = _iota((L, L), 1)
    lower = (c <= r)
    upper = (c >= r)
    m_col = (upper if rev else lower).astype(BF16)
    m_row = (lower if rev else upper).astype(BF16)
    lf_col = _log_sigmoid(gcol)
    lf_row = _log_sigmoid(grow)
    b_col = _mask_dot(m_col, lf_col)
    b_row = _dot_mask(lf_row, m_row)
    tot = jnp.sum(lf_row, axis=1, keepdims=True)
    causal = upper if rev else lower
    return b_col, b_row, tot, causal


def _mlstm_chunk(q, k, v, gcol, grow, sums, x_state, m, h, d, with_out):
    b_col_all, b_row_all, tot_all, causal = sums
    L = q.shape[0]
    gi = d * N_HEADS + h
    gf = (2 + d) * N_HEADS + h
    ig_c = gcol[:, gi:gi + 1]
    ig_r = grow[gi:gi + 1, :]
    b_c = b_col_all[:, gf:gf + 1]
    b_r = b_row_all[gf:gf + 1, :]
    bl = tot_all[gf:gf + 1, :]
    ones = jnp.ones((L, HEAD_DIM), BF16)
    v1 = jnp.concatenate([v, ones], axis=1)
    hout = None
    if with_out:
        logd = jnp.where(causal, b_c - b_r + ig_r, -jnp.inf)
        m_inter = b_c + m
        m_t = jnp.maximum(jnp.max(logd, axis=1, keepdims=True), m_inter)
        s = _dot_nt(q, k) * jnp.exp(logd - m_t)
        tot = _dot(s.astype(BF16), v1) + jnp.exp(m_inter - m_t) * _dot(q, x_state.astype(BF16))
        num = tot[:, :HEAD_DIM]
        den = tot[:, HEAD_DIM:]
        hout = num / jnp.maximum(jnp.abs(den), jnp.exp(-m_t))
    logw = bl - b_c + ig_c
    m_new = jnp.maximum(bl + m, jnp.max(logw, axis=0, keepdims=True))
    w_c = jnp.exp(logw - m_new)
    decay = jnp.exp(bl + m - m_new)
    vw = (v1.astype(F32) * w_c).astype(BF16)
    x_new = decay * x_state + _dot_tn(k, vw)
    return hout, x_new, m_new


def _mlstm_kernel(q_ref, k_ref, v_ref, mo_ref, gc_ref, gr_ref,
                  qc_ref, kc_ref, vc_ref, gcc_ref, grc_ref, nw_ref,
                  o_ref, x_scr, hf_scr, hb_scr):
    L = MLSTM_CHUNK
    t = q_ref.shape[0]
    tc = qc_ref.shape[0]
    nc, ncc = t // L, tc // L
    x_scr[...] = jnp.zeros_like(x_scr)

    def step(refs, n_chunks, with_out, i, ms):
        qr, kr, vr, gcr, grr = refs
        new_ms = []
        for d in range(2):
            ci = (n_chunks - 1 - i) if d else i
            off = pl.multiple_of(ci * L, L)
            gcol = gcr[pl.ds(off, L), :]
            grow = grr[0, ci]
            sums = _mlstm_gate_sums(gcol, grow, bool(d))
            for h in range(N_HEADS):
                hs = slice(h * HEAD_DIM, (h + 1) * HEAD_DIM)
                q = qr[pl.ds(off, L), hs]
                k = kr[pl.ds(off, L), hs]
                v = vr[pl.ds(off, L), hs]
                idx = d * N_HEADS + h
                hout, x_new, m_new = _mlstm_chunk(q, k, v, gcol, grow, sums, x_scr[idx], ms[idx], h, d, with_out)
                x_scr[idx] = x_new
                new_ms.append(m_new)
                if with_out:
                    (hb_scr if d else hf_scr)[pl.ds(off, L), hs] = hout
        return tuple(new_ms)

    ms0 = tuple(jnp.zeros((1, 1), F32) for _ in range(2 * N_HEADS))
    ctx_refs = (qc_ref, kc_ref, vc_ref, gcc_ref, grc_ref)
    lat_refs = (q_ref, k_ref, v_ref, gc_ref, gr_ref)
    ms = lax.fori_loop(0, ncc, functools.partial(step, ctx_refs, ncc, False), ms0)
    lax.fori_loop(0, nc, functools.partial(step, lat_refs, nc, True), ms)

    def epilogue(i, carry):
        off = pl.multiple_of(i * L, L)
        for h in range(N_HEADS):
            hs = slice(h * HEAD_DIM, (h + 1) * HEAD_DIM)
            o = hf_scr[pl.ds(off, L), hs] + hb_scr[pl.ds(off, L), hs]
            y = _rms(o, nw_ref[:, hs])
            o_ref[pl.ds(off, L), hs] = (y * _sigmoid(mo_ref[pl.ds(off, L), hs].astype(F32))).astype(o_ref.dtype)
        return carry

    lax.fori_loop(0, nc, epilogue, 0)


def _mlstm(qk_l, p16_l, p32_l, grow_l, qk_c, p16_c, p32_c, grow_c, m_norm, batch, t, tc):
    L = MLSTM_CHUNK
    w = W_MIX
    gcol_blk = p32_l.shape[1] // LANES - 1
    lat = lambda col: pl.BlockSpec((t, w), lambda b: (b, col))
    ctx = lambda col: pl.BlockSpec((tc, w), lambda b: (b, col))
    return pl.pallas_call(
        _mlstm_kernel,
        grid=(batch,),
        in_specs=[lat(0), lat(1), lat(P16_V), lat(P16_MO),
                  pl.BlockSpec((t, LANES), lambda b: (b, gcol_blk)),
                  pl.BlockSpec((1, t // L, 16, L), lambda b: (b, 0, 0, 0)),
                  ctx(0), ctx(1), ctx(P16_V),
                  pl.BlockSpec((tc, LANES), lambda b: (b, gcol_blk)),
                  pl.BlockSpec((1, tc // L, 16, L), lambda b: (b, 0, 0, 0)),
                  pl.BlockSpec((1, w), lambda b: (0, 0))],
        out_specs=pl.BlockSpec((t, w), lambda b: (b, 0)),
        out_shape=jax.ShapeDtypeStruct((batch * t, w), BF16),
        scratch_shapes=[pltpu.VMEM((2 * N_HEADS, HEAD_DIM, 2 * HEAD_DIM), F32),
                        pltpu.VMEM((t, w), F32), pltpu.VMEM((t, w), F32)],
        compiler_params=_cparams("parallel"),
        name="mlstm_scan",
    )(qk_l, qk_l, p16_l, p16_l, p32_l, grow_l, qk_c, qk_c, p16_c, p32_c, grow_c, m_norm)


def _hgrn_masks(L, c, rev):
    r = _iota((L, L), 0)
    u = _iota((L, L), 1)
    rb, ub = r // c, u // c
    n = L // c
    if not rev:
        mats = [u <= r,
                u > r,
                (ub == rb) & (u <= r),
                (ub == rb) & (u > r)]
        for i in range(2, n):
            mats.append((ub > rb) & (ub < i))
    else:
        mats = [u >= r, u < r, (ub == rb) & (u >= r), (ub == rb) & (u < r)]
        for i in range(0, n - 2):
            mats.append((ub < rb) & (ub > i))
    return jnp.concatenate([m.astype(BF16) for m in mats], axis=0)


def _hgrn_chunk(qv, kk, logf, iv, st, rev):
    L, c = HGRN_CHUNK, HGRN_SUB
    n = L // c
    pre = _mask_dot(_hgrn_masks(L, c, rev), logf)
    a_full = pre[0:L]
    ex = jnp.exp(pre)
    q_hat = (qv * ex[0:L]).astype(BF16)
    k_hat = (kk * ex[L:2 * L]).astype(BF16)
    q_t = (qv * ex[2 * L:3 * L]).astype(BF16)
    k_t = kk * ex[3 * L:4 * L]
    row = _iota((L, L), 0)
    lane = _iota((L, L), 1)
    blocks = []
    for i in range(n):
        qi = q_t[i * c:(i + 1) * c]
        has_keys = (i >= 1) if not rev else (i <= n - 2)
        if not has_keys:
            blocks.append(jnp.zeros((c, L), F32))
            continue
        adjacent_only = (i == 1) if not rev else (i == n - 2)
        if adjacent_only:
            kf = k_t
        else:
            j = (4 + (i - 2)) if not rev else (4 + i)
            kf = k_t * ex[j * L:(j + 1) * L]
        blocks.append(_dot_nt(qi, kf.astype(BF16)))
    s_cross = jnp.concatenate(blocks, axis=0)
    if not rev:
        far = (lane // c < row // c) & (row - lane >= c)
    else:
        far = (lane // c > row // c) & (lane - row >= c)
    s_mat = jnp.where(far, s_cross, 0.0)
    t_idx = _iota((L, HEAD_DIM), 0)
    es = []
    for dl in range(c):
        if dl == 0:
            e = qv * kk
        else:
            sh = (L - dl) if rev else dl
            a_sh = pltpu.roll(a_full, sh, 0)
            k_sh = pltpu.roll(kk, sh, 0)
            ok = (t_idx + dl < L) if rev else (t_idx >= dl)
            e = qv * k_sh * jnp.exp(jnp.where(ok, a_full - a_sh, 0.0))
        es.append(e.astype(BF16))
    rs = _dot(jnp.concatenate(es, axis=0), jnp.ones((HEAD_DIM, L), BF16))
    for dl in range(c):
        sel = (lane == row + dl) if rev else (lane == row - dl)
        s_mat = s_mat + jnp.where(sel, rs[dl * L:(dl + 1) * L], 0.0)
    o = _dot(s_mat.astype(BF16), iv) + _dot_nt(q_hat, st.astype(BF16))
    st_new = jnp.exp(jnp.sum(logf, axis=0, keepdims=True)) * st + _dot_tn(iv, k_hat)
    return o, st_new


def _hgrn_kernel(q_ref, i_ref, g_ref, zf_ref, zb_ref, qc_ref, ic_ref, zfc_ref, zbc_ref, lb_ref, nw_ref,
                 o_ref, st_scr, of_scr, ob_scr):
    L = HGRN_CHUNK
    t = q_ref.shape[0]
    tc = qc_ref.shape[0]
    nc, ncc = t // L, tc // L
    st_scr[...] = jnp.zeros_like(st_scr)

    def step(refs, n_chunks, with_out, i, carry):
        qr, ir, zrs = refs
        for d in range(2):
            ci = (n_chunks - 1 - i) if d else i
            off = pl.multiple_of(ci * L, L)
            for h in range(N_HEADS):
                hs = slice(h * HEAD_DIM, (h + 1) * HEAD_DIM)
                z = zrs[d][pl.ds(off, L), hs]
                lbv = lb_ref[d:d + 1, hs]
                e = jnp.exp(-jnp.abs(z))
                inv = 1.0 / (1.0 + e)
                sig = jnp.where(z >= 0, 1.0, e) * inv
                sig_n = jnp.where(z >= 0, e, 1.0) * inv
                logf = jnp.log(lbv + (1.0 - lbv) * sig)
                kk = (1.0 - lbv) * sig_n
                qf = qr[pl.ds(off, L), hs].astype(F32)
                qv = qf * _sigmoid(qf)
                iv = ir[pl.ds(off, L), hs]
                idx = d * N_HEADS + h
                o, st_new = _hgrn_chunk(qv, kk, logf, iv, st_scr[idx], bool(d))
                st_scr[idx] = st_new
                if with_out:
                    (ob_scr if d else of_scr)[pl.ds(off, L), hs] = o
        return carry

    lax.fori_loop(0, ncc, functools.partial(step, (qc_ref, ic_ref, (zfc_ref, zbc_ref)), ncc, False), 0)
    lax.fori_loop(0, nc, functools.partial(step, (q_ref, i_ref, (zf_ref, zb_ref)), nc, True), 0)

    def epilogue(i, carry):
        off = pl.multiple_of(i * L, L)
        for h in range(N_HEADS):
            hs = slice(h * HEAD_DIM, (h + 1) * HEAD_DIM)
            o = of_scr[pl.ds(off, L), hs] + ob_scr[pl.ds(off, L), hs]
            y = _rms(o, nw_ref[:, hs])
            g = g_ref[pl.ds(off, L), hs].astype(F32)
            o_ref[pl.ds(off, L), hs] = (y * (g * _sigmoid(g))).astype(o_ref.dtype)
        return carry

    lax.fori_loop(0, nc, epilogue, 0)


def _hgrn(p16_l, p32_l, p16_c, p32_c, lb, h_norm, batch, t, tc):
    w = W_MIX
    lat = lambda col: pl.BlockSpec((t, w), lambda b: (b, col))
    ctx = lambda col: pl.BlockSpec((tc, w), lambda b: (b, col))
    return pl.pallas_call(
        _hgrn_kernel,
        grid=(batch,),
        in_specs=[lat(P16_HQ), lat(P16_HI), lat(P16_HG), lat(0), lat(1),
                  ctx(P16_HQ), ctx(P16_HI), ctx(0), ctx(1),
                  pl.BlockSpec((2, w), lambda b: (0, 0)),
                  pl.BlockSpec((1, w), lambda b: (0, 0))],
        out_specs=pl.BlockSpec((t, w), lambda b: (b, 0)),
        out_shape=jax.ShapeDtypeStruct((batch * t, w), BF16),
        scratch_shapes=[pltpu.VMEM((2 * N_HEADS, HEAD_DIM, HEAD_DIM), F32),
                        pltpu.VMEM((t, w), F32), pltpu.VMEM((t, w), F32)],
        compiler_params=_cparams("parallel"),
        name="hgrn_scan",
    )(p16_l, p16_l, p16_l, p32_l, p32_l, p16_c, p16_c, p32_c, p32_c, lb, h_norm)


def _merge_kernel(a_ref, b_ref, ga_ref, gb_ref, x_ref, g1_ref, sh2_ref, sc2_ref, npost_ref, npre_ref,
                  wpa_ref, wpb_ref, wout_ref, wr_ref, br_ref,
                  x1_ref, h2_ref, te_ref, tw_ref, rk_ref, cnt_ref, run_scr):
    i = pl.program_id(0)
    tm = x_ref.shape[0]

    @pl.when(i == 0)
    def _():
        run_scr[...] = jnp.zeros_like(run_scr)

    ya = _dot(a_ref[...], wpa_ref[...])
    yb = _dot(b_ref[...], wpb_ref[...])
    y1 = _sigmoid(ga_ref[...].astype(F32)) * ya + _sigmoid(gb_ref[...].astype(F32)) * yb
    y = _dot(y1.astype(BF16), wout_ref[...])
    x1 = x_ref[...] + g1_ref[0] * _rms(y, npost_ref[...])
    x1_ref[...] = x1
    h2 = _rms(x1, npre_ref[...]) * (1.0 + sc2_ref[0]) + sh2_ref[0]
    h2_ref[...] = h2

    logits = _dot_f32(h2, wr_ref[...]) + br_ref[...]
    lane = _iota((tm, LANES), 1)
    vals, idxs = [], []
    lg = logits
    for _ in range(TOP_K):
        mx = jnp.max(lg, axis=1, keepdims=True)
        ix = jnp.min(jnp.where(lg == mx, lane, LANES), axis=1, keepdims=True)
        vals.append(mx)
        idxs.append(ix)
        lg = jnp.where(lane == ix, -jnp.inf, lg)
    es = [jnp.exp(v - vals[0]) for v in vals]
    den = es[0] + es[1] + es[2] + es[3]

    hits = [lane == ix for ix in idxs]
    cnt = sum(hh.astype(F32) for hh in hits)
    strict = (_iota((tm, tm), 1) < _iota((tm, tm), 0)).astype(BF16)
    base = _dot(strict, cnt.astype(BF16)) + run_scr[...]
    te = jnp.zeros((tm, LANES), I32)
    tw = jnp.zeros((tm, LANES), F32)
    rk = jnp.zeros((tm, LANES), I32)
    for k in range(TOP_K):
        rank = jnp.sum(jnp.where(hits[k], base, 0.0), axis=1, keepdims=True)
        te = jnp.where(lane == k, idxs[k], te)
        tw = jnp.where(lane == k, es[k] / den, tw)
        rk = jnp.where(lane == k, rank.astype(I32), rk)
    te_ref[...] = te
    tw_ref[...] = tw
    rk_ref[...] = rk
    run_scr[...] = run_scr[...] + jnp.sum(cnt, axis=0, keepdims=True)
    cnt_ref[...] = run_scr[...]


def _merge_route(a, b, p16, x2d, g1, sh2, sc2, npost, npre, wpa, wpb, wout, wr, br, t):
    m, d = x2d.shape
    tm = min(512, t)
    per = t // tm
    ga_blk = 1
    row = lambda width, col=0: pl.BlockSpec((tm, width), lambda i: (i, col))
    full = lambda r, c: pl.BlockSpec((r, c), lambda i: (0, 0))
    mod = pl.BlockSpec((1, 1, d), lambda i: (i // per, 0, 0))
    tok = lambda dt: jax.ShapeDtypeStruct((m, LANES), dt)
    return pl.pallas_call(
        _merge_kernel,
        grid=(m // tm,),
        in_specs=[row(W_MIX), row(W_MIX), row(d, ga_blk), row(d, ga_blk + 1), row(d),
                  mod, mod, mod, full(1, d), full(1, d),
                  full(W_MIX, d), full(W_MIX, d), full(d, d), full(d, LANES), full(1, LANES)],
        out_specs=[row(d), row(d), row(LANES), row(LANES), row(LANES), full(1, LANES)],
        out_shape=[jax.ShapeDtypeStruct((m, d), F32), jax.ShapeDtypeStruct((m, d), F32),
                   tok(I32), tok(F32), tok(I32), jax.ShapeDtypeStruct((1, LANES), F32)],
        scratch_shapes=[pltpu.VMEM((1, LANES), F32)],
        compiler_params=_cparams("arbitrary"),
        name="merge_route",
    )(a, b, p16, p16, x2d, g1, sh2, sc2, npost, npre, wpa, wpb, wout, wr, br)


def _dispatch_kernel(lo_ref, hi_ref, dest_ref, h_ref, xb_ref, sem, zsem, z_scr):
    i = pl.program_id(0)
    tm = h_ref.shape[0]

    def issue(r, carry):
        for k in range(TOP_K):
            dst = dest_ref[0, 0, r * TOP_K + k]
            pltpu.make_async_copy(h_ref.at[pl.ds(r, 1)], xb_ref.at[pl.ds(dst, 1)], sem).start()
        return carry

    lax.fori_loop(0, tm, issue, 0)

    @pl.when(i == 0)
    def _():
        z_scr[...] = jnp.zeros_like(z_scr)
        for e in range(N_EXPERTS):
            def pad_row(r, carry):
                cp = pltpu.make_async_copy(z_scr, xb_ref.at[pl.ds(r, 1)], zsem)
                cp.start()
                cp.wait()
                return carry
            lax.fori_loop(lo_ref[e], hi_ref[e], pad_row, 0)

    def drain(r, carry):
        for k in range(TOP_K):
            pltpu.make_async_copy(h_ref.at[pl.ds(0, 1)], xb_ref.at[pl.ds(0, 1)], sem).wait()
        return carry

    lax.fori_loop(0, tm, drain, 0)


def _dispatch(h2, dest, pad_lo, pad_hi, p_rows):
    m, d = h2.shape
    tm = 256
    nb = m // tm
    return pl.pallas_call(
        _dispatch_kernel,
        grid_spec=pltpu.PrefetchScalarGridSpec(
            num_scalar_prefetch=2,
            grid=(nb,),
            in_specs=[pl.BlockSpec((1, 1, tm * TOP_K), lambda i, lo, hi: (i, 0, 0), memory_space=pltpu.SMEM),
                      pl.BlockSpec((tm, d), lambda i, lo, hi: (i, 0))],
            out_specs=pl.BlockSpec(memory_space=pl.ANY),
            scratch_shapes=[pltpu.SemaphoreType.DMA, pltpu.SemaphoreType.DMA, pltpu.VMEM((1, d), F32)]),
        out_shape=jax.ShapeDtypeStruct((p_rows, d), F32),
        compiler_params=_cparams("arbitrary"),
        name="moe_dispatch",
    )(pad_lo, pad_hi, dest.reshape(nb, 1, tm * TOP_K), h2)


def _expert_kernel(be_ref, na_ref, x_ref, wgu_ref, bgu_ref, wdn_ref, bdn_ref, y_ref):
    i = pl.program_id(0)
    f = wdn_ref.shape[1]

    @pl.when(i < na_ref[0])
    def _():
        gu = _dot(x_ref[...].astype(BF16), wgu_ref[0]) + bgu_ref[0]
        gate = jnp.minimum(gu[:, :f], SWIGLU_LIMIT)
        up = jnp.clip(gu[:, f:], -SWIGLU_LIMIT, SWIGLU_LIMIT)
        act = (up + 1.0) * gate * _sigmoid(SWIGLU_ALPHA * gate)
        y_ref[...] = _dot(act.astype(BF16), wdn_ref[0]) + bdn_ref[0]

    @pl.when(i >= na_ref[0])
    def _():
        y_ref[...] = jnp.zeros_like(y_ref)


def _experts(xb, block_e, n_active, wgu, bgu, wdn, bdn):
    p_rows, d = xb.shape
    e, _, f2 = wgu.shape
    f = f2 // 2
    blk = MOE_BLOCK
    nb = p_rows // blk
    return pl.pallas_call(
        _expert_kernel,
        grid_spec=pltpu.PrefetchScalarGridSpec(
            num_scalar_prefetch=2,
            grid=(nb,),
            in_specs=[pl.BlockSpec((blk, d), lambda i, be, na: (jnp.minimum(i, na[0] - 1), 0)),
                      pl.BlockSpec((1, d, f2), lambda i, be, na: (be[i], 0, 0)),
                      pl.BlockSpec((1, 1, f2), lambda i, be, na: (be[i], 0, 0)),
                      pl.BlockSpec((1, f, d), lambda i, be, na: (be[i], 0, 0)),
                      pl.BlockSpec((1, 1, d), lambda i, be, na: (be[i], 0, 0))],
            out_specs=pl.BlockSpec((blk, d), lambda i, be, na: (i, 0))),
        out_shape=jax.ShapeDtypeStruct((p_rows, d), F32),
        compiler_params=_cparams("arbitrary"),
        name="moe_experts",
    )(block_e, n_active, xb, wgu, bgu.reshape(e, 1, f2), wdn, bdn.reshape(e, 1, d))


def _combine_kernel(dest_ref, yb_ref, tw_ref, x1_ref, g2_ref, nw_ref, o_ref, buf, sem):
    tm = x1_ref.shape[0]

    def issue(r, carry):
        for k in range(TOP_K):
            src = dest_ref[0, 0, r * TOP_K + k]
            pltpu.make_async_copy(yb_ref.at[pl.ds(src, 1)], buf.at[k, pl.ds(r, 1)], sem).start()
        return carry

    lax.fori_loop(0, tm, issue, 0)

    def drain(r, carry):
        for k in range(TOP_K):
            pltpu.make_async_copy(yb_ref.at[pl.ds(0, 1)], buf.at[0, pl.ds(0, 1)], sem).wait()
        return carry

    lax.fori_loop(0, tm, drain, 0)
    tw = tw_ref[...]
    y = buf[0] * tw[:, 0:1]
    for k in range(1, TOP_K):
        y = y + buf[k] * tw[:, k:k + 1]
    o_ref[...] = x1_ref[...] + g2_ref[0] * _rms(y, nw_ref[...])


def _combine(yb, dest, tw, x1, g2, nw, t):
    m, d = x1.shape
    tm = 256
    per = t // tm
    nb = m // tm
    return pl.pallas_call(
        _combine_kernel,
        grid=(nb,),
        in_specs=[pl.BlockSpec((1, 1, tm * TOP_K), lambda i: (i, 0, 0), memory_space=pltpu.SMEM),
                  pl.BlockSpec(memory_space=pl.ANY),
                  pl.BlockSpec((tm, LANES), lambda i: (i, 0)),
                  pl.BlockSpec((tm, d), lambda i: (i, 0)),
                  pl.BlockSpec((1, 1, d), lambda i: (i // per, 0, 0)),
                  pl.BlockSpec((1, d), lambda i: (0, 0))],
        out_specs=pl.BlockSpec((tm, d), lambda i: (i, 0)),
        out_shape=jax.ShapeDtypeStruct((m, d), F32),
        scratch_shapes=[pltpu.VMEM((TOP_K, tm, d), F32), pltpu.SemaphoreType.DMA],
        compiler_params=_cparams("arbitrary"),
        name="moe_combine",
    )(dest.reshape(nb, 1, tm * TOP_K), yb, tw, x1, g2, nw)


def _gate_rows(p32, batch, t, L):
    g = p32[:, 2 * W_MIX:2 * W_MIX + 4 * N_HEADS].reshape(batch, t // L, L, 4 * N_HEADS)
    return jnp.swapaxes(g, 2, 3)


def kernel(x, c, ctx, c_ctx, w_ada, b_ada, norm_mix_pre, norm_mix_post, norm_ffn_pre, norm_ffn_post, w_in, b_in, conv_w, lb_raw, m_norm, h_norm, w_pa, w_pb, w_out, w_router, b_router, w_gu, b_gu, w_dn, b_dn):
    assert w_in.shape[0] == 1, "single-layer kernel: context tokens only hand over scan states"
    batch, t, d = x.shape
    tc = ctx.shape[1]
    w = W_MIX
    n_tok = batch * t

    rows = -(-(batch + 1) // 8) * 8
    c_all = jnp.zeros((rows, d), F32).at[:batch].set(c).at[batch].set(c_ctx)
    mod = _ada_mod(c_all, w_ada[0], b_ada[0])
    sh1, sc1, g1, sh2, sc2, g2 = [mod[:, i * d:(i + 1) * d] for i in range(6)]
    per_b = lambda a: a[:batch].reshape(batch, 1, d)
    per_c = lambda a: a[batch:batch + 1].reshape(1, 1, d)

    wi, bi = w_in[0], b_in[0]
    hf0, hf1 = 2 * w + 5 * w, 2 * w + 7 * w
    g0 = hf1 + 2 * d
    w16 = jnp.concatenate([wi[:, :2 * w], wi[:, hf1:g0], wi[:, 2 * w:hf0]], axis=1).astype(BF16)
    b16 = jnp.concatenate([bi[:2 * w], bi[hf1:g0], bi[2 * w:hf0]]).reshape(1, -1)
    n32 = 2 * w + LANES
    w32 = jnp.zeros((d, n32), F32).at[:, :2 * w].set(wi[:, hf0:hf1]).at[:, 2 * w:2 * w + 4 * N_HEADS].set(wi[:, g0:]).astype(BF16)
    b32 = jnp.zeros((n32,), F32).at[:2 * w].set(bi[hf0:hf1]).at[2 * w:2 * w + 4 * N_HEADS].set(bi[g0:]).reshape(1, -1)

    x2d = x.reshape(n_tok, d)
    c2d = ctx.reshape(batch * tc, d)
    nmp = norm_mix_pre[0].reshape(1, d)
    n16 = w16.shape[1]
    tn16 = n16 // 4
    p16_l = _proj(x2d, nmp, per_b(sc1), per_b(sh1), w16, b16, t, BF16, tn16)
    p32_l = _proj(x2d, nmp, per_b(sc1), per_b(sh1), w32, b32, t, F32, n32)
    p16_c = _proj(c2d, nmp, per_c(sc1), per_c(sh1), w16, b16, batch * tc, BF16, tn16)
    p32_c = _proj(c2d, nmp, per_c(sc1), per_c(sh1), w32, b32, batch * tc, F32, n32)

    cw9 = jnp.zeros((16, 2 * w), F32).at[:9].set(conv_w[0].reshape(9, 2 * w))
    qk_l = _conv_silu(p16_l, cw9, batch, t, GRID_W)
    qk_c = _conv_silu(p16_c, cw9, batch, tc, tc)

    a = _mlstm(qk_l, p16_l, p32_l, _gate_rows(p32_l, batch, t, MLSTM_CHUNK),
               qk_c, p16_c, p32_c, _gate_rows(p32_c, batch, tc, MLSTM_CHUNK),
               m_norm[0].reshape(1, w), batch, t, tc)

    lb_all = jnp.cumsum(jax.nn.softmax(lb_raw.astype(F32), axis=0), axis=0)
    b = _hgrn(p16_l, p32_l, p16_c, p32_c, lb_all[0], h_norm[0].reshape(1, w), batch, t, tc)

    wr = jnp.zeros((d, LANES), F32).at[:, :N_EXPERTS].set(w_router[0])
    br = jnp.full((1, LANES), NEG_BIG, F32).at[0, :N_EXPERTS].set(b_router[0])
    x1, h2, te, tw, rk, cnt = _merge_route(
        a, b, p16_l, x2d, per_b(g1), per_b(sh2), per_b(sc2),
        norm_mix_post[0].reshape(1, d), norm_ffn_pre[0].reshape(1, d),
        w_pa[0].astype(BF16), w_pb[0].astype(BF16), w_out[0].astype(BF16), wr, br, t)

    blk = MOE_BLOCK
    counts = cnt[0, :N_EXPERTS].astype(I32)
    pcounts = (counts + blk - 1) // blk * blk
    pend = jnp.cumsum(pcounts)
    pstart = pend - pcounts
    dest = (pstart[te[:, :TOP_K]] + rk[:, :TOP_K]).reshape(-1)
    p_rows = -(-n_tok * TOP_K // blk) * blk + N_EXPERTS * blk
    n_blocks = p_rows // blk
    block_e = jnp.minimum(jnp.searchsorted(pend, jnp.arange(n_blocks, dtype=I32) * blk, side='right'),
                          N_EXPERTS - 1).astype(I32)
    n_active = (pend[-1:] // blk).astype(I32)

    xb = _dispatch(h2, dest, pstart + counts, pend, p_rows)
    yb = _experts(xb, block_e, n_active, w_gu[0].astype(BF16), b_gu[0], w_dn[0].astype(BF16), b_dn[0])
    out = _combine(yb, dest, tw, x1, per_b(g2), norm_ffn_post[0].reshape(1, d), t)
    return out.reshape(batch, t, d)
```

```python
import functools

import jax
import jax.numpy as jnp
from jax import lax
from jax.experimental import pallas as pl
from jax.experimental.pallas import tpu as pltpu

F32 = jnp.float32
BF16 = jnp.bfloat16
I32 = jnp.int32

EPS = 1e-6
N_HEADS = 4
HEAD_DIM = 128
W_MIX = N_HEADS * HEAD_DIM
GRID_W = 64
N_EXPERTS = 32
TOP_K = 4
SWIGLU_LIMIT = 7.0
SWIGLU_ALPHA = 1.702
LANES = 128
VMEM_LIMIT = 56 * 1024 * 1024

MLSTM_CHUNK = 128
HGRN_CHUNK = 64
HGRN_SUB = 8
MOE_BLOCK = 256
NEG_BIG = -1e30
P16_V, P16_MO, P16_HQ, P16_HI, P16_HG = 6, 7, 8, 9, 10


def _cparams(*sem):
    return pltpu.CompilerParams(dimension_semantics=sem, vmem_limit_bytes=VMEM_LIMIT)


def _dot(a, b):
    return jnp.dot(a, b, preferred_element_type=F32)


def _dot_nt(a, b):
    return lax.dot_general(a, b, (((1,), (1,)), ((), ())), preferred_element_type=F32)


def _dot_tn(a, b):
    return lax.dot_general(a, b, (((0,), (0,)), ((), ())), preferred_element_type=F32)


def _split3(x):
    x1 = x.astype(BF16)
    r1 = x - x1.astype(F32)
    x2 = r1.astype(BF16)
    r2 = r1 - x2.astype(F32)
    x3 = r2.astype(BF16)
    return x1, x2, x3


def _mask_dot(mask_bf16, x):
    x1, x2, x3 = _split3(x)
    return _dot(mask_bf16, x1) + _dot(mask_bf16, x2) + _dot(mask_bf16, x3)


def _dot_mask(x, mask_bf16):
    x1, x2, x3 = _split3(x)
    return _dot(x1, mask_bf16) + _dot(x2, mask_bf16) + _dot(x3, mask_bf16)


def _dot_f32(a, b):
    a1, a2, a3 = _split3(a)
    b1, b2, b3 = _split3(b)
    return (_dot(a1, b1) + (_dot(a1, b2) + _dot(a2, b1))
            + (_dot(a2, b2) + _dot(a1, b3) + _dot(a3, b1)))


def _sigmoid(x):
    return 1.0 / (1.0 + jnp.exp(-x))


def _log_sigmoid(x):
    return jnp.minimum(x, 0.0) - jnp.log(1.0 + jnp.exp(-jnp.abs(x)))


def _rms(xf, w):
    return xf * lax.rsqrt(jnp.mean(xf * xf, axis=-1, keepdims=True) + EPS) * w


def _iota(shape, dim):
    return lax.broadcasted_iota(I32, shape, dim)


def _ada_kernel(c_ref, w_ref, b_ref, o_ref):
    cv = c_ref[...]
    s = cv * _sigmoid(cv)
    o_ref[...] = _dot_f32(s, w_ref[...]) + b_ref[...]


def _ada_mod(c_all, w, b):
    rows, d = c_all.shape
    n = w.shape[1]
    tn = 1536
    return pl.pallas_call(
        _ada_kernel,
        grid=(n // tn,),
        in_specs=[pl.BlockSpec((rows, d), lambda j: (0, 0)),
                  pl.BlockSpec((d, tn), lambda j: (0, j)),
                  pl.BlockSpec((1, tn), lambda j: (0, j))],
        out_specs=pl.BlockSpec((rows, tn), lambda j: (0, j)),
        out_shape=jax.ShapeDtypeStruct((rows, n), F32),
        compiler_params=_cparams("parallel"),
        name="ada_mod",
    )(c_all, w, b.reshape(1, n))


def _proj_kernel(x_ref, nw_ref, sc_ref, sh_ref, w_ref, b_ref, o_ref, h_scr):
    @pl.when(pl.program_id(1) == 0)
    def _():
        h = _rms(x_ref[...], nw_ref[...]) * (1.0 + sc_ref[0]) + sh_ref[0]
        h_scr[...] = h.astype(BF16)

    o_ref[...] = (_dot(h_scr[...], w_ref[...]) + b_ref[...]).astype(o_ref.dtype)


def _proj(x2d, nw, sc, sh, w, b, tokens_per_mod, out_dtype, tn):
    m, d = x2d.shape
    n = w.shape[1]
    tm = min(1024, tokens_per_mod)
    per = tokens_per_mod // tm
    return pl.pallas_call(
        _proj_kernel,
        grid=(m // tm, n // tn),
        in_specs=[pl.BlockSpec((tm, d), lambda i, j: (i, 0)),
                  pl.BlockSpec((1, d), lambda i, j: (0, 0)),
                  pl.BlockSpec((1, 1, d), lambda i, j: (i // per, 0, 0)),
                  pl.BlockSpec((1, 1, d), lambda i, j: (i // per, 0, 0)),
                  pl.BlockSpec((d, tn), lambda i, j: (0, j)),
                  pl.BlockSpec((1, tn), lambda i, j: (0, j))],
        out_specs=pl.BlockSpec((tm, tn), lambda i, j: (i, j)),
        out_shape=jax.ShapeDtypeStruct((m, n), out_dtype),
        scratch_shapes=[pltpu.VMEM((tm, d), BF16)],
        compiler_params=_cparams("parallel", "arbitrary"),
        name="in_proj",
    )(x2d, nw, sc, sh, w, b)


def _conv_kernel(x_ref, w_ref, o_ref, *, cols, q_blocks):
    t, c = x_ref.shape
    x = x_ref[...].astype(F32)
    col = lax.rem(_iota((t, c), 0), cols)
    xm = jnp.where(col == 0, 0.0, pltpu.roll(x, 1, 0))
    xp = jnp.where(col == cols - 1, 0.0, pltpu.roll(x, t - 1, 0))

    def hrow(dr):
        return w_ref[3 * dr:3 * dr + 1, :] * xm + w_ref[3 * dr + 1:3 * dr + 2, :] * x + w_ref[3 * dr + 2:3 * dr + 3, :] * xp

    y = hrow(1)
    if t > cols:
        z = jnp.zeros((cols, c), F32)
        y = y + jnp.concatenate([z, hrow(0)[:t - cols]], axis=0) + jnp.concatenate([hrow(2)[cols:], z], axis=0)
    y = y * _sigmoid(y)
    scale = jnp.where(pl.program_id(1) < q_blocks, HEAD_DIM ** -0.5, 1.0)
    o_ref[...] = (y * scale).astype(o_ref.dtype)


def _conv_silu(p16, conv_w9, batch, t, cols):
    cw = 256
    nblk = 2 * W_MIX // cw
    return pl.pallas_call(
        functools.partial(_conv_kernel, cols=cols, q_blocks=W_MIX // cw),
        grid=(batch, nblk),
        in_specs=[pl.BlockSpec((t, cw), lambda b, j: (b, j)),
                  pl.BlockSpec((16, cw), lambda b, j: (0, j))],
        out_specs=pl.BlockSpec((t, cw), lambda b, j: (b, j)),
        out_shape=jax.ShapeDtypeStruct((batch * t, 2 * W_MIX), BF16),
        compiler_params=_cparams("parallel", "parallel"),
        name="conv_silu",
    )(p16, conv_w9)


def _mlstm_gate_sums(gcol, grow, rev):
    L = gcol.shape[0]
    r = _iota((L, L), 0)
    c = _iota((L, L), 1)
    lower = (c <= r)
    upper = (c >= r)
    m_col = (upper if rev else lower).astype(BF16)
    m_row = (lower if rev else upper).astype(BF16)
    lf_col = _log_sigmoid(gcol)
    lf_row = _log_sigmoid(grow)
    b_col = _mask_dot(m_col, lf_col)
    b_row = _dot_mask(lf_row, m_row)
    tot = jnp.sum(lf_row, axis=1, keepdims=True)
    causal = upper if rev else lower
    return b_col, b_row, tot, causal


def _mlstm_stage_state(q, k, v, gcol, grow, sums, x_state, m, h, d, with_out):
    b_col_all, b_row_all, tot_all, causal = sums
    L = q.shape[0]
    gi = d * N_HEADS + h
    gf = (2 + d) * N_HEADS + h
    ig_c = gcol[:, gi:gi + 1]
    b_c = b_col_all[:, gf:gf + 1]
    bl = tot_all[gf:gf + 1, :]
    v1 = jnp.concatenate([v, jnp.ones((L, HEAD_DIM), BF16)], axis=1)
    part = None
    if with_out:
        logd = jnp.where(causal, b_c - b_row_all[gf:gf + 1, :] + grow[gi:gi + 1, :], -jnp.inf)
        m_inter = b_c + m
        m_t = jnp.maximum(jnp.max(logd, axis=1, keepdims=True), m_inter)
        part = dict(v1=v1, m_t=m_t, decay_mask=jnp.exp(logd - m_t), scores=_dot_nt(q, k),
                    inter=jnp.exp(m_inter - m_t) * _dot(q, x_state.astype(BF16)))
    logw = bl - b_c + ig_c
    m_new = jnp.maximum(bl + m, jnp.max(logw, axis=0, keepdims=True))
    w_c = jnp.exp(logw - m_new)
    decay = jnp.exp(bl + m - m_new)
    vw = (v1.astype(F32) * w_c).astype(BF16)
    x_new = decay * x_state + _dot_tn(k, vw)
    return part, x_new, m_new


def _mlstm_stage_out(part):
    s = part['scores'] * part['decay_mask']
    tot = _dot(s.astype(BF16), part['v1']) + part['inter']
    num = tot[:, :HEAD_DIM]
    den = tot[:, HEAD_DIM:]
    return num / jnp.maximum(jnp.abs(den), jnp.exp(-part['m_t']))


def _mlstm_kernel(q_ref, k_ref, v_ref, mo_ref, gc_ref, gr_ref,
                  qc_ref, kc_ref, vc_ref, gcc_ref, grc_ref, nw_ref,
                  o_ref, x_scr, hf_scr, hb_scr):
    L = MLSTM_CHUNK
    t = q_ref.shape[0]
    tc = qc_ref.shape[0]
    nc, ncc = t // L, tc // L
    x_scr[...] = jnp.zeros_like(x_scr)

    def step(refs, n_chunks, with_out, i, ms):
        qr, kr, vr, gcr, grr = refs
        new_ms, parts = [], []
        cis = [(n_chunks - 1 - i) if d else i for d in range(2)]
        offs = [pl.multiple_of(ci * L, L) for ci in cis]
        gcols = [gcr[pl.ds(offs[d], L), :] for d in range(2)]
        grows = [grr[0, cis[d]] for d in range(2)]
        sums = [_mlstm_gate_sums(gcols[d], grows[d], bool(d)) for d in range(2)]
        for d in range(2):
            for h in range(N_HEADS):
                hs = slice(h * HEAD_DIM, (h + 1) * HEAD_DIM)
                q = qr[pl.ds(offs[d], L), hs]
                k = kr[pl.ds(offs[d], L), hs]
                v = vr[pl.ds(offs[d], L), hs]
                idx = d * N_HEADS + h
                part, x_new, m_new = _mlstm_stage_state(q, k, v, gcols[d], grows[d], sums[d], x_scr[idx], ms[idx],
                                                        h, d, with_out)
                x_scr[idx] = x_new
                new_ms.append(m_new)
                parts.append(part)
        if with_out:
            for d in range(2):
                for h in range(N_HEADS):
                    hs = slice(h * HEAD_DIM, (h + 1) * HEAD_DIM)
                    (hb_scr if d else hf_scr)[pl.ds(offs[d], L), hs] = _mlstm_stage_out(parts[d * N_HEADS + h])
        return tuple(new_ms)

    ms0 = tuple(jnp.zeros((1, 1), F32) for _ in range(2 * N_HEADS))
    ctx_refs = (qc_ref, kc_ref, vc_ref, gcc_ref, grc_ref)
    lat_refs = (q_ref, k_ref, v_ref, gc_ref, gr_ref)
    ms = lax.fori_loop(0, ncc, functools.partial(step, ctx_refs, ncc, False), ms0)
    lax.fori_loop(0, nc, functools.partial(step, lat_refs, nc, True), ms)

    def epilogue(i, carry):
        off = pl.multiple_of(i * L, L)
        for h in range(N_HEADS):
            hs = slice(h * HEAD_DIM, (h + 1) * HEAD_DIM)
            o = hf_scr[pl.ds(off, L), hs] + hb_scr[pl.ds(off, L), hs]
            y = _rms(o, nw_ref[:, hs])
            o_ref[pl.ds(off, L), hs] = (y * _sigmoid(mo_ref[pl.ds(off, L), hs].astype(F32))).astype(o_ref.dtype)
        return carry

    lax.fori_loop(0, nc, epilogue, 0)


def _mlstm(qk_l, p16_l, p32_l, grow_l, qk_c, p16_c, p32_c, grow_c, m_norm, batch, t, tc):
    L = MLSTM_CHUNK
    w = W_MIX
    gcol_blk = p32_l.shape[1] // LANES - 1
    lat = lambda col: pl.BlockSpec((t, w), lambda b: (b, col))
    ctx = lambda col: pl.BlockSpec((tc, w), lambda b: (b, col))
    return pl.pallas_call(
        _mlstm_kernel,
        grid=(batch,),
        in_specs=[lat(0), lat(1), lat(P16_V), lat(P16_MO),
                  pl.BlockSpec((t, LANES), lambda b: (b, gcol_blk)),
                  pl.BlockSpec((1, t // L, 16, L), lambda b: (b, 0, 0, 0)),
                  ctx(0), ctx(1), ctx(P16_V),
                  pl.BlockSpec((tc, LANES), lambda b: (b, gcol_blk)),
                  pl.BlockSpec((1, tc // L, 16, L), lambda b: (b, 0, 0, 0)),
                  pl.BlockSpec((1, w), lambda b: (0, 0))],
        out_specs=pl.BlockSpec((t, w), lambda b: (b, 0)),
        out_shape=jax.ShapeDtypeStruct((batch * t, w), BF16),
        scratch_shapes=[pltpu.VMEM((2 * N_HEADS, HEAD_DIM, 2 * HEAD_DIM), F32),
                        pltpu.VMEM((t, w), F32), pltpu.VMEM((t, w), F32)],
        compiler_params=_cparams("parallel"),
        name="mlstm_scan",
    )(qk_l, qk_l, p16_l, p16_l, p32_l, grow_l, qk_c, qk_c, p16_c, p32_c, grow_c, m_norm)


def _hgrn_masks(L, c, rev):
    r = _iota((L, L), 0)
    u = _iota((L, L), 1)
    rb, ub = r // c, u // c
    n = L // c
    if not rev:
        mats = [u <= r,
                u > r,
                (ub == rb) & (u <= r),
                (ub == rb) & (u > r)]
        for i in range(2, n):
            mats.append((ub > rb) & (ub < i))
    else:
        mats = [u >= r, u < r, (ub == rb) & (u >= r), (ub == rb) & (u < r)]
        for i in range(0, n - 2):
            mats.append((ub < rb) & (ub > i))
    return jnp.concatenate([m.astype(BF16) for m in mats], axis=0)


def _hgrn_band_masks(L, c, rev):
    row = _iota((L, L), 0)
    lane = _iota((L, L), 1)
    dist = (lane - row) if rev else (row - lane)
    ms = [(dist == dl) for dl in range(c)]
    ms.append((dist >= c) & (lane // c != row // c))
    return jnp.stack([m.astype(F32) for m in ms], axis=0)


def _hgrn_stage_gates(z, lbv, qf, masks):
    L = HGRN_CHUNK
    e = jnp.exp(-jnp.abs(z))
    inv = 1.0 / (1.0 + e)
    sig = jnp.where(z >= 0, 1.0, e) * inv
    sig_n = jnp.where(z >= 0, e, 1.0) * inv
    f = lbv + (1.0 - lbv) * sig
    logf = jnp.log(f)
    kk = (1.0 - lbv) * sig_n
    qv = qf * _sigmoid(qf)
    x1, x2, x3 = _split3(logf)
    parts = jnp.concatenate([x1, x2, x3, jnp.zeros_like(x1)], axis=0)
    ex = jnp.exp(_dot(masks, parts))
    g = dict(f=f, kk=kk, qv=qv, ex=ex, decay=jnp.exp(jnp.sum(logf, axis=0, keepdims=True)),
             q_hat=(qv * ex[0:L]).astype(BF16), k_hat=(kk * ex[L:2 * L]).astype(BF16))
    if masks.shape[0] > 2 * L:
        g.update(q_t=(qv * ex[2 * L:3 * L]).astype(BF16), k_t=kk * ex[3 * L:4 * L])
    return g


def _hgrn_stage_cross(g, hs, rev):
    L, c = HGRN_CHUNK, HGRN_SUB
    n = L // c
    blocks = []
    for i in range(n):
        has_keys = (i >= 1) if not rev else (i <= n - 2)
        if not has_keys:
            blocks.append(jnp.zeros((c, L), F32))
            continue
        adjacent_only = (i == 1) if not rev else (i == n - 2)
        kf = g['k_t'][:, hs]
        if not adjacent_only:
            j = (4 + (i - 2)) if not rev else (4 + i)
            kf = kf * g['ex'][j * L:(j + 1) * L, hs]
        blocks.append(_dot_nt(g['q_t'][i * c:(i + 1) * c, hs], kf.astype(BF16)))
    return jnp.concatenate(blocks, axis=0)


def _hgrn_stage_band(g, rev):
    L, c = HGRN_CHUNK, HGRN_SUB
    p = g['kk']
    es = [(g['qv'] * p).astype(BF16)]
    for dl in range(1, c):
        p = g['f'] * pltpu.roll(p, (L - 1) if rev else 1, 0)
        es.append((g['qv'] * p).astype(BF16))
    return jnp.concatenate(es, axis=0)


def _hgrn_kernel(q_ref, i_ref, g_ref, zf_ref, zb_ref, qc_ref, ic_ref, zfc_ref, zbc_ref, lb_ref, nw_ref,
                 o_ref, st_scr, of_scr, ob_scr, mask_scr, band_scr):
    L = HGRN_CHUNK
    t = q_ref.shape[0]
    tc = qc_ref.shape[0]
    nc, ncc = t // L, tc // L
    st_scr[...] = jnp.zeros_like(st_scr)
    for d in range(2):
        m01 = _hgrn_masks(L, HGRN_SUB, bool(d))
        mask_scr[d] = jnp.concatenate([m01, m01, m01, jnp.zeros_like(m01)], axis=1)
        band_scr[d] = _hgrn_band_masks(L, HGRN_SUB, bool(d))

    def step(refs, n_chunks, with_out, i, carry):
        qr, ir, zrs = refs
        c = HGRN_SUB
        heads = [slice(h * HEAD_DIM, (h + 1) * HEAD_DIM) for h in range(N_HEADS)]
        offs = [pl.multiple_of(((n_chunks - 1 - i) if d else i) * L, L) for d in range(2)]
        gs = [_hgrn_stage_gates(zrs[d][pl.ds(offs[d], L), :], lb_ref[d:d + 1, :],
                                qr[pl.ds(offs[d], L), :].astype(F32),
                                mask_scr[d] if with_out else mask_scr[d, 0:2 * L]) for d in range(2)]
        ivs = [ir[pl.ds(offs[d], L), :] for d in range(2)]
        for d in range(2):
            for h, hs in enumerate(heads):
                idx = d * N_HEADS + h
                st = st_scr[idx]
                st_scr[idx] = gs[d]['decay'][:, hs] * st + _dot_tn(ivs[d][:, hs], gs[d]['k_hat'][:, hs])
                if with_out:
                    (ob_scr if d else of_scr)[pl.ds(offs[d], L), hs] = _dot_nt(gs[d]['q_hat'][:, hs], st.astype(BF16))
        if not with_out:
            return carry
        cross = [[_hgrn_stage_cross(gs[d], hs, bool(d)) for hs in heads] for d in range(2)]
        bands = [_hgrn_stage_band(gs[d], bool(d)) for d in range(2)]
        ones = jnp.ones((HEAD_DIM, L), BF16)
        rss = [[_dot(bands[d][:, hs], ones) for hs in heads] for d in range(2)]
        for d in range(2):
            for h, hs in enumerate(heads):
                s_mat = cross[d][h] * band_scr[d, c]
                for dl in range(c):
                    s_mat = s_mat + rss[d][h][dl * L:(dl + 1) * L] * band_scr[d, dl]
                o_scr = ob_scr if d else of_scr
                o_scr[pl.ds(offs[d], L), hs] = o_scr[pl.ds(offs[d], L), hs] + _dot(s_mat.astype(BF16), ivs[d][:, hs])
        return carry

    lax.fori_loop(0, ncc, functools.partial(step, (qc_ref, ic_ref, (zfc_ref, zbc_ref)), ncc, False), 0)
    lax.fori_loop(0, nc, functools.partial(step, (q_ref, i_ref, (zf_ref, zb_ref)), nc, True), 0)

    def epilogue(i, carry):
        off = pl.multiple_of(i * L, L)
        for h in range(N_HEADS):
            hs = slice(h * HEAD_DIM, (h + 1) * HEAD_DIM)
            o = of_scr[pl.ds(off, L), hs] + ob_scr[pl.ds(off, L), hs]
            y = _rms(o, nw_ref[:, hs])
            g = g_ref[pl.ds(off, L), hs].astype(F32)
            o_ref[pl.ds(off, L), hs] = (y * (g * _sigmoid(g))).astype(o_ref.dtype)
        return carry

    lax.fori_loop(0, nc, epilogue, 0)


def _hgrn(p16_l, p32_l, p16_c, p32_c, lb, h_norm, batch, t, tc):
    w = W_MIX
    lat = lambda col: pl.BlockSpec((t, w), lambda b: (b, col))
    ctx = lambda col: pl.BlockSpec((tc, w), lambda b: (b, col))
    return pl.pallas_call(
        _hgrn_kernel,
        grid=(batch,),
        in_specs=[lat(P16_HQ), lat(P16_HI), lat(P16_HG), lat(0), lat(1),
                  ctx(P16_HQ), ctx(P16_HI), ctx(0), ctx(1),
                  pl.BlockSpec((2, w), lambda b: (0, 0)),
                  pl.BlockSpec((1, w), lambda b: (0, 0))],
        out_specs=pl.BlockSpec((t, w), lambda b: (b, 0)),
        out_shape=jax.ShapeDtypeStruct((batch * t, w), BF16),
        scratch_shapes=[pltpu.VMEM((2 * N_HEADS, HEAD_DIM, HEAD_DIM), F32),
                        pltpu.VMEM((t, w), F32), pltpu.VMEM((t, w), F32),
                        pltpu.VMEM((2, (2 + HGRN_CHUNK // HGRN_SUB) * HGRN_CHUNK, 4 * HGRN_CHUNK), BF16),
                        pltpu.VMEM((2, HGRN_SUB + 1, HGRN_CHUNK, HGRN_CHUNK), F32)],
        compiler_params=_cparams("parallel"),
        name="hgrn_scan",
    )(p16_l, p16_l, p16_l, p32_l, p32_l, p16_c, p16_c, p32_c, p32_c, lb, h_norm)


def _merge_kernel(a_ref, b_ref, ga_ref, gb_ref, x_ref, g1_ref, sh2_ref, sc2_ref, npost_ref, npre_ref,
                  wpa_ref, wpb_ref, wout_ref, wr_ref, br_ref,
                  x1_ref, h2_ref, te_ref, tw_ref, rk_ref, cnt_ref, run_scr):
    i = pl.program_id(0)
    tm = x_ref.shape[0]

    @pl.when(i == 0)
    def _():
        run_scr[...] = jnp.zeros_like(run_scr)

    ya = _dot(a_ref[...], wpa_ref[...])
    yb = _dot(b_ref[...], wpb_ref[...])
    y1 = _sigmoid(ga_ref[...].astype(F32)) * ya + _sigmoid(gb_ref[...].astype(F32)) * yb
    y = _dot(y1.astype(BF16), wout_ref[...])
    x1 = x_ref[...] + g1_ref[0] * _rms(y, npost_ref[...])
    x1_ref[...] = x1
    h2 = _rms(x1, npre_ref[...]) * (1.0 + sc2_ref[0]) + sh2_ref[0]
    h2_ref[...] = h2

    logits = _dot_f32(h2, wr_ref[...]) + br_ref[...]
    lane = _iota((tm, LANES), 1)
    vals, idxs = [], []
    lg = logits
    for _ in range(TOP_K):
        mx = jnp.max(lg, axis=1, keepdims=True)
        ix = jnp.min(jnp.where(lg == mx, lane, LANES), axis=1, keepdims=True)
        vals.append(mx)
        idxs.append(ix)
        lg = jnp.where(lane == ix, -jnp.inf, lg)
    es = [jnp.exp(v - vals[0]) for v in vals]
    den = es[0] + es[1] + es[2] + es[3]

    hits = [lane == ix for ix in idxs]
    cnt = sum(hh.astype(F32) for hh in hits)
    strict = (_iota((tm, tm), 1) < _iota((tm, tm), 0)).astype(BF16)
    base = _dot(strict, cnt.astype(BF16)) + run_scr[...]
    te = jnp.zeros((tm, LANES), I32)
    tw = jnp.zeros((tm, LANES), F32)
    rk = jnp.zeros((tm, LANES), I32)
    for k in range(TOP_K):
        rank = jnp.sum(jnp.where(hits[k], base, 0.0), axis=1, keepdims=True)
        te = jnp.where(lane == k, idxs[k], te)
        tw = jnp.where(lane == k, es[k] / den, tw)
        rk = jnp.where(lane == k, rank.astype(I32), rk)
    te_ref[...] = te
    tw_ref[...] = tw
    rk_ref[...] = rk
    run_scr[...] = run_scr[...] + jnp.sum(cnt, axis=0, keepdims=True)
    cnt_ref[...] = run_scr[...]


def _merge_route(a, b, p16, x2d, g1, sh2, sc2, npost, npre, wpa, wpb, wout, wr, br, t):
    m, d = x2d.shape
    tm = min(512, t)
    per = t // tm
    ga_blk = 1
    row = lambda width, col=0: pl.BlockSpec((tm, width), lambda i: (i, col))
    full = lambda r, c: pl.BlockSpec((r, c), lambda i: (0, 0))
    mod = pl.BlockSpec((1, 1, d), lambda i: (i // per, 0, 0))
    tok = lambda dt: jax.ShapeDtypeStruct((m, LANES), dt)
    return pl.pallas_call(
        _merge_kernel,
        grid=(m // tm,),
        in_specs=[row(W_MIX), row(W_MIX), row(d, ga_blk), row(d, ga_blk + 1), row(d),
                  mod, mod, mod, full(1, d), full(1, d),
                  full(W_MIX, d), full(W_MIX, d), full(d, d), full(d, LANES), full(1, LANES)],
        out_specs=[row(d), row(d), row(LANES), row(LANES), row(LANES), full(1, LANES)],
        out_shape=[jax.ShapeDtypeStruct((m, d), F32), jax.ShapeDtypeStruct((m, d), F32),
                   tok(I32), tok(F32), tok(I32), jax.ShapeDtypeStruct((1, LANES), F32)],
        scratch_shapes=[pltpu.VMEM((1, LANES), F32)],
        compiler_params=_cparams("arbitrary"),
        name="merge_route",
    )(a, b, p16, p16, x2d, g1, sh2, sc2, npost, npre, wpa, wpb, wout, wr, br)


def _dispatch_kernel(lo_ref, hi_ref, dest_ref, h_ref, xb_ref, sem, zsem, z_scr):
    i = pl.program_id(0)
    tm = h_ref.shape[0]

    def issue(r, carry):
        for k in range(TOP_K):
            dst = dest_ref[0, 0, r * TOP_K + k]
            pltpu.make_async_copy(h_ref.at[pl.ds(r, 1)], xb_ref.at[pl.ds(dst, 1)], sem).start()
        return carry

    lax.fori_loop(0, tm, issue, 0, unroll=8)

    @pl.when(i == 0)
    def _():
        z_scr[...] = jnp.zeros_like(z_scr)

        def pad_copy(r):
            return pltpu.make_async_copy(z_scr, xb_ref.at[pl.ds(r, 1)], zsem)

        for e in range(N_EXPERTS):
            lax.fori_loop(lo_ref[e], hi_ref[e], lambda r, carry: (pad_copy(r).start(), carry)[1], 0)
        for e in range(N_EXPERTS):
            lax.fori_loop(lo_ref[e], hi_ref[e], lambda r, carry: (pad_copy(r).wait(), carry)[1], 0)

    for k in range(TOP_K):
        pltpu.make_async_copy(h_ref, xb_ref.at[pl.ds(0, tm)], sem).wait()


def _dispatch(h2, dest, pad_lo, pad_hi, p_rows):
    m, d = h2.shape
    tm = 256
    nb = m // tm
    return pl.pallas_call(
        _dispatch_kernel,
        grid_spec=pltpu.PrefetchScalarGridSpec(
            num_scalar_prefetch=2,
            grid=(nb,),
            in_specs=[pl.BlockSpec((1, 1, tm * TOP_K), lambda i, lo, hi: (i, 0, 0), memory_space=pltpu.SMEM),
                      pl.BlockSpec((tm, d), lambda i, lo, hi: (i, 0))],
            out_specs=pl.BlockSpec(memory_space=pl.ANY),
            scratch_shapes=[pltpu.SemaphoreType.DMA, pltpu.SemaphoreType.DMA, pltpu.VMEM((1, d), F32)]),
        out_shape=jax.ShapeDtypeStruct((p_rows, d), F32),
        compiler_params=_cparams("arbitrary"),
        name="moe_dispatch",
    )(pad_lo, pad_hi, dest.reshape(nb, 1, tm * TOP_K), h2)


def _expert_kernel(be_ref, na_ref, x_ref, wgu_ref, bgu_ref, wdn_ref, bdn_ref, y_ref):
    i = pl.program_id(0)
    f = wdn_ref.shape[1]

    @pl.when(i < na_ref[0])
    def _():
        gu = _dot(x_ref[...].astype(BF16), wgu_ref[0]) + bgu_ref[0]
        gate = jnp.minimum(gu[:, :f], SWIGLU_LIMIT)
        up = jnp.clip(gu[:, f:], -SWIGLU_LIMIT, SWIGLU_LIMIT)
        act = (up + 1.0) * gate * _sigmoid(SWIGLU_ALPHA * gate)
        y_ref[...] = _dot(act.astype(BF16), wdn_ref[0]) + bdn_ref[0]

    @pl.when(i >= na_ref[0])
    def _():
        y_ref[...] = jnp.zeros_like(y_ref)


def _experts(xb, block_e, n_active, wgu, bgu, wdn, bdn):
    p_rows, d = xb.shape
    e, _, f2 = wgu.shape
    f = f2 // 2
    blk = MOE_BLOCK
    nb = p_rows // blk
    return pl.pallas_call(
        _expert_kernel,
        grid_spec=pltpu.PrefetchScalarGridSpec(
            num_scalar_prefetch=2,
            grid=(nb,),
            in_specs=[pl.BlockSpec((blk, d), lambda i, be, na: (jnp.minimum(i, na[0] - 1), 0)),
                      pl.BlockSpec((1, d, f2), lambda i, be, na: (be[i], 0, 0)),
                      pl.BlockSpec((1, 1, f2), lambda i, be, na: (be[i], 0, 0)),
                      pl.BlockSpec((1, f, d), lambda i, be, na: (be[i], 0, 0)),
                      pl.BlockSpec((1, 1, d), lambda i, be, na: (be[i], 0, 0))],
            out_specs=pl.BlockSpec((blk, d), lambda i, be, na: (i, 0))),
        out_shape=jax.ShapeDtypeStruct((p_rows, d), F32),
        compiler_params=_cparams("arbitrary"),
        name="moe_experts",
    )(block_e, n_active, xb, wgu, bgu.reshape(e, 1, f2), wdn, bdn.reshape(e, 1, d))


def _combine_kernel(dest_ref, yb_ref, tw_ref, x1_ref, g2_ref, nw_ref, o_ref, buf, sem):
    tm = x1_ref.shape[0]

    def issue(r, carry):
        for k in range(TOP_K):
            src = dest_ref[0, 0, r * TOP_K + k]
            pltpu.make_async_copy(yb_ref.at[pl.ds(src, 1)], buf.at[k, pl.ds(r, 1)], sem).start()
        return carry

    lax.fori_loop(0, tm, issue, 0, unroll=8)
    for k in range(TOP_K):
        pltpu.make_async_copy(yb_ref.at[pl.ds(0, tm)], buf.at[k], sem).wait()
    tw = tw_ref[...]
    y = buf[0] * tw[:, 0:1]
    for k in range(1, TOP_K):
        y = y + buf[k] * tw[:, k:k + 1]
    o_ref[...] = x1_ref[...] + g2_ref[0] * _rms(y, nw_ref[...])


def _combine(yb, dest, tw, x1, g2, nw, t):
    m, d = x1.shape
    tm = 256
    per = t // tm
    nb = m // tm
    return pl.pallas_call(
        _combine_kernel,
        grid=(nb,),
        in_specs=[pl.BlockSpec((1, 1, tm * TOP_K), lambda i: (i, 0, 0), memory_space=pltpu.SMEM),
                  pl.BlockSpec(memory_space=pl.ANY),
                  pl.BlockSpec((tm, LANES), lambda i: (i, 0)),
                  pl.BlockSpec((tm, d), lambda i: (i, 0)),
                  pl.BlockSpec((1, 1, d), lambda i: (i // per, 0, 0)),
                  pl.BlockSpec((1, d), lambda i: (0, 0))],
        out_specs=pl.BlockSpec((tm, d), lambda i: (i, 0)),
        out_shape=jax.ShapeDtypeStruct((m, d), F32),
        scratch_shapes=[pltpu.VMEM((TOP_K, tm, d), F32), pltpu.SemaphoreType.DMA],
        compiler_params=_cparams("arbitrary"),
        name="moe_combine",
    )(dest.reshape(nb, 1, tm * TOP_K), yb, tw, x1, g2, nw)


def _gate_rows(p32, batch, t, L):
    g = p32[:, 2 * W_MIX:2 * W_MIX + 4 * N_HEADS].reshape(batch, t // L, L, 4 * N_HEADS)
    return jnp.swapaxes(g, 2, 3)


def kernel(x, c, ctx, c_ctx, w_ada, b_ada, norm_mix_pre, norm_mix_post, norm_ffn_pre, norm_ffn_post, w_in, b_in, conv_w, lb_raw, m_norm, h_norm, w_pa, w_pb, w_out, w_router, b_router, w_gu, b_gu, w_dn, b_dn):
    assert w_in.shape[0] == 1, "single-layer kernel: context tokens only hand over scan states"
    batch, t, d = x.shape
    tc = ctx.shape[1]
    w = W_MIX
    n_tok = batch * t

    rows = -(-(batch + 1) // 8) * 8
    c_all = jnp.zeros((rows, d), F32).at[:batch].set(c).at[batch].set(c_ctx)
    mod = _ada_mod(c_all, w_ada[0], b_ada[0])
    sh1, sc1, g1, sh2, sc2, g2 = [mod[:, i * d:(i + 1) * d] for i in range(6)]
    per_b = lambda a: a[:batch].reshape(batch, 1, d)
    per_c = lambda a: a[batch:batch + 1].reshape(1, 1, d)

    wi, bi = w_in[0], b_in[0]
    hf0, hf1 = 2 * w + 5 * w, 2 * w + 7 * w
    g0 = hf1 + 2 * d
    w16 = jnp.concatenate([wi[:, :2 * w], wi[:, hf1:g0], wi[:, 2 * w:hf0]], axis=1).astype(BF16)
    b16 = jnp.concatenate([bi[:2 * w], bi[hf1:g0], bi[2 * w:hf0]]).reshape(1, -1)
    n32 = 2 * w + LANES
    w32 = jnp.zeros((d, n32), F32).at[:, :2 * w].set(wi[:, hf0:hf1]).at[:, 2 * w:2 * w + 4 * N_HEADS].set(wi[:, g0:]).astype(BF16)
    b32 = jnp.zeros((n32,), F32).at[:2 * w].set(bi[hf0:hf1]).at[2 * w:2 * w + 4 * N_HEADS].set(bi[g0:]).reshape(1, -1)

    x2d = x.reshape(n_tok, d)
    c2d = ctx.reshape(batch * tc, d)
    nmp = norm_mix_pre[0].reshape(1, d)
    n16 = w16.shape[1]
    tn16 = n16 // 4
    p16_l = _proj(x2d, nmp, per_b(sc1), per_b(sh1), w16, b16, t, BF16, tn16)
    p32_l = _proj(x2d, nmp, per_b(sc1), per_b(sh1), w32, b32, t, F32, n32)
    p16_c = _proj(c2d, nmp, per_c(sc1), per_c(sh1), w16, b16, batch * tc, BF16, tn16)
    p32_c = _proj(c2d, nmp, per_c(sc1), per_c(sh1), w32, b32, batch * tc, F32, n32)

    cw9 = jnp.zeros((16, 2 * w), F32).at[:9].set(conv_w[0].reshape(9, 2 * w))
    qk_l = _conv_silu(p16_l, cw9, batch, t, GRID_W)
    qk_c = _conv_silu(p16_c, cw9, batch, tc, tc)

    a = _mlstm(qk_l, p16_l, p32_l, _gate_rows(p32_l, batch, t, MLSTM_CHUNK),
               qk_c, p16_c, p32_c, _gate_rows(p32_c, batch, tc, MLSTM_CHUNK),
               m_norm[0].reshape(1, w), batch, t, tc)

    lb_all = jnp.cumsum(jax.nn.softmax(lb_raw.astype(F32), axis=0), axis=0)
    b = _hgrn(p16_l, p32_l, p16_c, p32_c, lb_all[0], h_norm[0].reshape(1, w), batch, t, tc)

    wr = jnp.zeros((d, LANES), F32).at[:, :N_EXPERTS].set(w_router[0])
    br = jnp.full((1, LANES), NEG_BIG, F32).at[0, :N_EXPERTS].set(b_router[0])
    x1, h2, te, tw, rk, cnt = _merge_route(
        a, b, p16_l, x2d, per_b(g1), per_b(sh2), per_b(sc2),
        norm_mix_post[0].reshape(1, d), norm_ffn_pre[0].reshape(1, d),
        w_pa[0].astype(BF16), w_pb[0].astype(BF16), w_out[0].astype(BF16), wr, br, t)

    blk = MOE_BLOCK
    counts = cnt[0, :N_EXPERTS].astype(I32)
    pcounts = (counts + blk - 1) // blk * blk
    pend = jnp.cumsum(pcounts)
    pstart = pend - pcounts
    dest = (pstart[te[:, :TOP_K]] + rk[:, :TOP_K]).reshape(-1)
    p_rows = -(-n_tok * TOP_K // blk) * blk + N_EXPERTS * blk
    n_blocks = p_rows // blk
    block_row0 = jnp.arange(n_blocks, dtype=I32) * blk
    block_e = jnp.minimum(jnp.sum((pend[None, :] <= block_row0[:, None]).astype(I32), axis=1), N_EXPERTS - 1)
    n_active = (pend[-1:] // blk).astype(I32)

    xb = _dispatch(h2, dest, pstart + counts, pend, p_rows)
    yb = _experts(xb, block_e, n_active, w_gu[0].astype(BF16), b_gu[0], w_dn[0].astype(BF16), b_dn[0])
    out = _combine(yb, dest, tw, x1, per_b(g2), norm_ffn_post[0].reshape(1, d), t)
    return out.reshape(batch, t, d)
```

```python
import functools

import jax
import jax.numpy as jnp
from jax import lax
from jax.experimental import pallas as pl
from jax.experimental.pallas import tpu as pltpu

F32 = jnp.float32
BF16 = jnp.bfloat16
I32 = jnp.int32

EPS = 1e-6
N_HEADS = 4
HEAD_DIM = 128
W_MIX = N_HEADS * HEAD_DIM
GRID_W = 64
N_EXPERTS = 32
TOP_K = 4
SWIGLU_LIMIT = 7.0
SWIGLU_ALPHA = 1.702
LANES = 128
VMEM_LIMIT = 56 * 1024 * 1024

MLSTM_CHUNK = 128
HGRN_CHUNK = 64
HGRN_SUB = 8
MOE_BLOCK = 512
NEG_BIG = -1e30
P16_V, P16_MO, P16_HQ, P16_HI, P16_HG = 6, 7, 8, 9, 10


def _cparams(*sem):
    return pltpu.CompilerParams(dimension_semantics=sem, vmem_limit_bytes=VMEM_LIMIT)


def _dot(a, b):
    return jnp.dot(a, b, preferred_element_type=F32)


def _dot_nt(a, b):
    return lax.dot_general(a, b, (((1,), (1,)), ((), ())), preferred_element_type=F32)


def _dot_tn(a, b):
    return lax.dot_general(a, b, (((0,), (0,)), ((), ())), preferred_element_type=F32)


def _split3(x):
    x1 = x.astype(BF16)
    r1 = x - x1.astype(F32)
    x2 = r1.astype(BF16)
    r2 = r1 - x2.astype(F32)
    x3 = r2.astype(BF16)
    return x1, x2, x3


def _mask_dot(mask_bf16, x):
    x1, x2, x3 = _split3(x)
    return _dot(mask_bf16, x1) + _dot(mask_bf16, x2) + _dot(mask_bf16, x3)


def _dot_mask(x, mask_bf16):
    x1, x2, x3 = _split3(x)
    return _dot(x1, mask_bf16) + _dot(x2, mask_bf16) + _dot(x3, mask_bf16)


def _dot_f32(a, b):
    a1, a2, a3 = _split3(a)
    b1, b2, b3 = _split3(b)
    return (_dot(a1, b1) + (_dot(a1, b2) + _dot(a2, b1))
            + (_dot(a2, b2) + _dot(a1, b3) + _dot(a3, b1)))


def _sigmoid(x):
    return 1.0 / (1.0 + jnp.exp(-x))


def _log_sigmoid(x):
    return jnp.minimum(x, 0.0) - jnp.log(1.0 + jnp.exp(-jnp.abs(x)))


def _rms(xf, w):
    return xf * lax.rsqrt(jnp.mean(xf * xf, axis=-1, keepdims=True) + EPS) * w


def _iota(shape, dim):
    return lax.broadcasted_iota(I32, shape, dim)


def _ada_kernel(c_ref, w_ref, b_ref, o_ref):
    cv = c_ref[...]
    s = cv * _sigmoid(cv)
    o_ref[...] = _dot_f32(s, w_ref[...]) + b_ref[...]


def _ada_mod(c_all, w, b):
    rows, d = c_all.shape
    n = w.shape[1]
    tn = 1536
    return pl.pallas_call(
        _ada_kernel,
        grid=(n // tn,),
        in_specs=[pl.BlockSpec((rows, d), lambda j: (0, 0)),
                  pl.BlockSpec((d, tn), lambda j: (0, j)),
                  pl.BlockSpec((1, tn), lambda j: (0, j))],
        out_specs=pl.BlockSpec((rows, tn), lambda j: (0, j)),
        out_shape=jax.ShapeDtypeStruct((rows, n), F32),
        compiler_params=_cparams("parallel"),
        name="ada_mod",
    )(c_all, w, b.reshape(1, n))


def _proj_kernel(x_ref, nw_ref, sc_ref, sh_ref, w_ref, b_ref, o_ref, h_scr):
    @pl.when(pl.program_id(1) == 0)
    def _():
        h = _rms(x_ref[...], nw_ref[...]) * (1.0 + sc_ref[0]) + sh_ref[0]
        h_scr[...] = h.astype(BF16)

    o_ref[...] = (_dot(h_scr[...], w_ref[...]) + b_ref[...]).astype(o_ref.dtype)


def _proj(x2d, nw, sc, sh, w, b, tokens_per_mod, out_dtype, tn):
    m, d = x2d.shape
    n = w.shape[1]
    tm = min(1024, tokens_per_mod)
    per = tokens_per_mod // tm
    return pl.pallas_call(
        _proj_kernel,
        grid=(m // tm, n // tn),
        in_specs=[pl.BlockSpec((tm, d), lambda i, j: (i, 0)),
                  pl.BlockSpec((1, d), lambda i, j: (0, 0)),
                  pl.BlockSpec((1, 1, d), lambda i, j: (i // per, 0, 0)),
                  pl.BlockSpec((1, 1, d), lambda i, j: (i // per, 0, 0)),
                  pl.BlockSpec((d, tn), lambda i, j: (0, j)),
                  pl.BlockSpec((1, tn), lambda i, j: (0, j))],
        out_specs=pl.BlockSpec((tm, tn), lambda i, j: (i, j)),
        out_shape=jax.ShapeDtypeStruct((m, n), out_dtype),
        scratch_shapes=[pltpu.VMEM((tm, d), BF16)],
        compiler_params=_cparams("parallel", "arbitrary"),
        name="in_proj",
    )(x2d, nw, sc, sh, w, b)


def _conv_kernel(x_ref, w_ref, o_ref, *, cols, q_blocks):
    t, c = x_ref.shape
    x = x_ref[...].astype(F32)
    col = lax.rem(_iota((t, c), 0), cols)
    xm = jnp.where(col == 0, 0.0, pltpu.roll(x, 1, 0))
    xp = jnp.where(col == cols - 1, 0.0, pltpu.roll(x, t - 1, 0))

    def hrow(dr):
        return w_ref[3 * dr:3 * dr + 1, :] * xm + w_ref[3 * dr + 1:3 * dr + 2, :] * x + w_ref[3 * dr + 2:3 * dr + 3, :] * xp

    y = hrow(1)
    if t > cols:
        z = jnp.zeros((cols, c), F32)
        y = y + jnp.concatenate([z, hrow(0)[:t - cols]], axis=0) + jnp.concatenate([hrow(2)[cols:], z], axis=0)
    y = y * _sigmoid(y)
    scale = jnp.where(pl.program_id(1) < q_blocks, HEAD_DIM ** -0.5, 1.0)
    o_ref[...] = (y * scale).astype(o_ref.dtype)


def _conv_silu(p16, conv_w9, batch, t, cols):
    cw = 256
    nblk = 2 * W_MIX // cw
    return pl.pallas_call(
        functools.partial(_conv_kernel, cols=cols, q_blocks=W_MIX // cw),
        grid=(batch, nblk),
        in_specs=[pl.BlockSpec((t, cw), lambda b, j: (b, j)),
                  pl.BlockSpec((16, cw), lambda b, j: (0, j))],
        out_specs=pl.BlockSpec((t, cw), lambda b, j: (b, j)),
        out_shape=jax.ShapeDtypeStruct((batch * t, 2 * W_MIX), BF16),
        compiler_params=_cparams("parallel", "parallel"),
        name="conv_silu",
    )(p16, conv_w9)


def _mlstm_gate_sums(gcol, grow, rev):
    L = gcol.shape[0]
    r = _iota((L, L), 0)
    c = _iota((L, L), 1)
    lower = (c <= r)
    upper = (c >= r)
    m_col = (upper if rev else lower).astype(BF16)
    m_row = (lower if rev else upper).astype(BF16)
    lf_col = _log_sigmoid(gcol)
    lf_row = _log_sigmoid(grow)
    b_col = _mask_dot(m_col, lf_col)
    b_row = _dot_mask(lf_row, m_row)
    tot = jnp.sum(lf_row, axis=1, keepdims=True)
    causal = upper if rev else lower
    return b_col, b_row, tot, causal


def _mlstm_stage_state(q, k, v, gcol, grow, sums, x_state, m, h, d, with_out):
    b_col_all, b_row_all, tot_all, causal = sums
    L = q.shape[0]
    gi = d * N_HEADS + h
    gf = (2 + d) * N_HEADS + h
    assert L == HEAD_DIM
    ig_c = jnp.broadcast_to(gcol[:, gi:gi + 1], (L, HEAD_DIM))
    b_c = jnp.broadcast_to(b_col_all[:, gf:gf + 1], (L, HEAD_DIM))
    bl = jnp.broadcast_to(tot_all[gf:gf + 1, :], (1, HEAD_DIM))
    twice = lambda a: jnp.concatenate([a, a], axis=1)
    v1 = jnp.concatenate([v, jnp.ones((L, HEAD_DIM), BF16)], axis=1)
    part = None
    if with_out:
        logd = jnp.where(causal, b_c - b_row_all[gf:gf + 1, :] + grow[gi:gi + 1, :], -jnp.inf)
        m_inter = b_c + m
        m_t = jnp.maximum(jnp.max(logd, axis=1, keepdims=True), m_inter)
        part = dict(v1=v1, m_t=m_t, decay_mask=jnp.exp(logd - m_t), scores=_dot_nt(q, k),
                    inter=twice(jnp.exp(m_inter - m_t)) * _dot(q, x_state.astype(BF16)))
    logw = bl - b_c + ig_c
    m_new = jnp.maximum(bl + m, jnp.max(logw, axis=0, keepdims=True))
    w_c = jnp.exp(logw - m_new)
    decay = jnp.exp(bl + m - m_new)
    vw = (v1.astype(F32) * twice(w_c)).astype(BF16)
    x_new = twice(decay) * x_state + _dot_tn(k, vw)
    return part, x_new, m_new


def _mlstm_stage_out(part):
    s = part['scores'] * part['decay_mask']
    tot = _dot(s.astype(BF16), part['v1']) + part['inter']
    num = tot[:, :HEAD_DIM]
    den = tot[:, HEAD_DIM:]
    return num / jnp.maximum(jnp.abs(den), jnp.exp(-part['m_t']))


def _mlstm_kernel(q_ref, k_ref, v_ref, mo_ref, gc_ref, gr_ref,
                  qc_ref, kc_ref, vc_ref, gcc_ref, grc_ref, nw_ref,
                  o_ref, x_scr, hf_scr, hb_scr):
    L = MLSTM_CHUNK
    t = q_ref.shape[0]
    tc = qc_ref.shape[0]
    nc, ncc = t // L, tc // L
    x_scr[...] = jnp.zeros_like(x_scr)

    def step(refs, n_chunks, with_out, i, ms):
        qr, kr, vr, gcr, grr = refs
        new_ms, parts = [], []
        cis = [(n_chunks - 1 - i) if d else i for d in range(2)]
        offs = [pl.multiple_of(ci * L, L) for ci in cis]
        gcols = [gcr[pl.ds(offs[d], L), :] for d in range(2)]
        grows = [grr[0, cis[d]] for d in range(2)]
        sums = [_mlstm_gate_sums(gcols[d], grows[d], bool(d)) for d in range(2)]
        for d in range(2):
            for h in range(N_HEADS):
                hs = slice(h * HEAD_DIM, (h + 1) * HEAD_DIM)
                q = qr[pl.ds(offs[d], L), hs]
                k = kr[pl.ds(offs[d], L), hs]
                v = vr[pl.ds(offs[d], L), hs]
                idx = d * N_HEADS + h
                part, x_new, m_new = _mlstm_stage_state(q, k, v, gcols[d], grows[d], sums[d], x_scr[idx], ms[idx],
                                                        h, d, with_out)
                x_scr[idx] = x_new
                new_ms.append(m_new)
                parts.append(part)
        if with_out:
            for d in range(2):
                for h in range(N_HEADS):
                    hs = slice(h * HEAD_DIM, (h + 1) * HEAD_DIM)
                    (hb_scr if d else hf_scr)[pl.ds(offs[d], L), hs] = _mlstm_stage_out(parts[d * N_HEADS + h])
        return tuple(new_ms)

    ms0 = tuple(jnp.zeros((1, HEAD_DIM), F32) for _ in range(2 * N_HEADS))
    ctx_refs = (qc_ref, kc_ref, vc_ref, gcc_ref, grc_ref)
    lat_refs = (q_ref, k_ref, v_ref, gc_ref, gr_ref)
    ms = lax.fori_loop(0, ncc, functools.partial(step, ctx_refs, ncc, False), ms0)
    lax.fori_loop(0, nc, functools.partial(step, lat_refs, nc, True), ms)

    def epilogue(i, carry):
        off = pl.multiple_of(i * L, L)
        for h in range(N_HEADS):
            hs = slice(h * HEAD_DIM, (h + 1) * HEAD_DIM)
            o = hf_scr[pl.ds(off, L), hs] + hb_scr[pl.ds(off, L), hs]
            y = _rms(o, nw_ref[:, hs])
            o_ref[pl.ds(off, L), hs] = (y * _sigmoid(mo_ref[pl.ds(off, L), hs].astype(F32))).astype(o_ref.dtype)
        return carry

    lax.fori_loop(0, nc, epilogue, 0)


def _mlstm(qk_l, p16_l, p32_l, grow_l, qk_c, p16_c, p32_c, grow_c, m_norm, batch, t, tc):
    L = MLSTM_CHUNK
    w = W_MIX
    gcol_blk = p32_l.shape[1] // LANES - 1
    lat = lambda col: pl.BlockSpec((t, w), lambda b: (b, col))
    ctx = lambda col: pl.BlockSpec((tc, w), lambda b: (b, col))
    return pl.pallas_call(
        _mlstm_kernel,
        grid=(batch,),
        in_specs=[lat(0), lat(1), lat(P16_V), lat(P16_MO),
                  pl.BlockSpec((t, LANES), lambda b: (b, gcol_blk)),
                  pl.BlockSpec((1, t // L, 16, L), lambda b: (b, 0, 0, 0)),
                  ctx(0), ctx(1), ctx(P16_V),
                  pl.BlockSpec((tc, LANES), lambda b: (b, gcol_blk)),
                  pl.BlockSpec((1, tc // L, 16, L), lambda b: (b, 0, 0, 0)),
                  pl.BlockSpec((1, w), lambda b: (0, 0))],
        out_specs=pl.BlockSpec((t, w), lambda b: (b, 0)),
        out_shape=jax.ShapeDtypeStruct((batch * t, w), BF16),
        scratch_shapes=[pltpu.VMEM((2 * N_HEADS, HEAD_DIM, 2 * HEAD_DIM), F32),
                        pltpu.VMEM((t, w), F32), pltpu.VMEM((t, w), F32)],
        compiler_params=_cparams("parallel"),
        name="mlstm_scan",
    )(qk_l, qk_l, p16_l, p16_l, p32_l, grow_l, qk_c, qk_c, p16_c, p32_c, grow_c, m_norm)


def _hgrn_masks(L, c, rev):
    r = _iota((L, L), 0)
    u = _iota((L, L), 1)
    rb, ub = r // c, u // c
    n = L // c
    if not rev:
        mats = [u <= r,
                u > r,
                (ub == rb) & (u <= r),
                (ub == rb) & (u > r)]
        for i in range(2, n):
            mats.append((ub > rb) & (ub < i))
    else:
        mats = [u >= r, u < r, (ub == rb) & (u >= r), (ub == rb) & (u < r)]
        for i in range(0, n - 2):
            mats.append((ub < rb) & (ub > i))
    return jnp.concatenate([m.astype(BF16) for m in mats], axis=0)


def _hgrn_band_masks(L, c, rev):
    row = _iota((L, L), 0)
    lane = _iota((L, L), 1)
    dist = (lane - row) if rev else (row - lane)
    ms = [(dist == dl) for dl in range(c)]
    ms.append((dist >= c) & (lane // c != row // c))
    return jnp.stack([m.astype(F32) for m in ms], axis=0)


def _hgrn_stage_gates(z, lbv, qf, masks):
    L = HGRN_CHUNK
    e = jnp.exp(-jnp.abs(z))
    inv = 1.0 / (1.0 + e)
    sig = jnp.where(z >= 0, 1.0, e) * inv
    sig_n = jnp.where(z >= 0, e, 1.0) * inv
    f = lbv + (1.0 - lbv) * sig
    logf = jnp.log(f)
    kk = (1.0 - lbv) * sig_n
    qv = qf * _sigmoid(qf)
    x1, x2, x3 = _split3(logf)
    parts = jnp.concatenate([x1, x2, x3, jnp.zeros_like(x1)], axis=0)
    ex = jnp.exp(_dot(masks, parts))
    g = dict(f=f, kk=kk, qv=qv, ex=ex, decay=jnp.exp(jnp.sum(logf, axis=0, keepdims=True)),
             q_hat=(qv * ex[0:L]).astype(BF16), k_hat=(kk * ex[L:2 * L]).astype(BF16))
    if masks.shape[0] > 2 * L:
        g.update(q_t=(qv * ex[2 * L:3 * L]).astype(BF16), k_t=kk * ex[3 * L:4 * L])
    return g


def _hgrn_stage_cross(g, hs, rev):
    L, c = HGRN_CHUNK, HGRN_SUB
    n = L // c
    blocks = []
    for i in range(n):
        has_keys = (i >= 1) if not rev else (i <= n - 2)
        if not has_keys:
            blocks.append(jnp.zeros((c, L), F32))
            continue
        adjacent_only = (i == 1) if not rev else (i == n - 2)
        kf = g['k_t'][:, hs]
        if not adjacent_only:
            j = (4 + (i - 2)) if not rev else (4 + i)
            kf = kf * g['ex'][j * L:(j + 1) * L, hs]
        blocks.append(_dot_nt(g['q_t'][i * c:(i + 1) * c, hs], kf.astype(BF16)))
    return jnp.concatenate(blocks, axis=0)


def _hgrn_stage_band(g, rev):
    L, c = HGRN_CHUNK, HGRN_SUB
    p = g['kk']
    es = [(g['qv'] * p).astype(BF16)]
    for dl in range(1, c):
        p = g['f'] * pltpu.roll(p, (L - 1) if rev else 1, 0)
        es.append((g['qv'] * p).astype(BF16))
    return jnp.concatenate(es, axis=0)


def _hgrn_kernel(q_ref, i_ref, g_ref, zf_ref, zb_ref, qc_ref, ic_ref, zfc_ref, zbc_ref, lb_ref, nw_ref,
                 o_ref, st_scr, of_scr, ob_scr, mask_scr, band_scr):
    L = HGRN_CHUNK
    t = q_ref.shape[0]
    tc = qc_ref.shape[0]
    nc, ncc = t // L, tc // L
    st_scr[...] = jnp.zeros_like(st_scr)
    for d in range(2):
        m01 = _hgrn_masks(L, HGRN_SUB, bool(d))
        mask_scr[d] = jnp.concatenate([m01, m01, m01, jnp.zeros_like(m01)], axis=1)
        band_scr[d] = _hgrn_band_masks(L, HGRN_SUB, bool(d))

    def step(refs, n_chunks, with_out, i, carry):
        qr, ir, zrs = refs
        c = HGRN_SUB
        heads = [slice(h * HEAD_DIM, (h + 1) * HEAD_DIM) for h in range(N_HEADS)]
        offs = [pl.multiple_of(((n_chunks - 1 - i) if d else i) * L, L) for d in range(2)]
        gs = [_hgrn_stage_gates(zrs[d][pl.ds(offs[d], L), :], lb_ref[d:d + 1, :],
                                qr[pl.ds(offs[d], L), :].astype(F32),
                                mask_scr[d] if with_out else mask_scr[d, 0:2 * L]) for d in range(2)]
        ivs = [ir[pl.ds(offs[d], L), :] for d in range(2)]
        for d in range(2):
            for h, hs in enumerate(heads):
                idx = d * N_HEADS + h
                st = st_scr[idx]
                st_scr[idx] = gs[d]['decay'][:, hs] * st + _dot_tn(ivs[d][:, hs], gs[d]['k_hat'][:, hs])
                if with_out:
                    (ob_scr if d else of_scr)[pl.ds(offs[d], L), hs] = _dot_nt(gs[d]['q_hat'][:, hs], st.astype(BF16))
        if not with_out:
            return carry
        cross = [[_hgrn_stage_cross(gs[d], hs, bool(d)) for hs in heads] for d in range(2)]
        bands = [_hgrn_stage_band(gs[d], bool(d)) for d in range(2)]
        ones = jnp.ones((HEAD_DIM, L), BF16)
        rss = [[_dot(bands[d][:, hs], ones) for hs in heads] for d in range(2)]
        for d in range(2):
            for h, hs in enumerate(heads):
                s_mat = cross[d][h] * band_scr[d, c]
                for dl in range(c):
                    s_mat = s_mat + rss[d][h][dl * L:(dl + 1) * L] * band_scr[d, dl]
                o_scr = ob_scr if d else of_scr
                o_scr[pl.ds(offs[d], L), hs] = o_scr[pl.ds(offs[d], L), hs] + _dot(s_mat.astype(BF16), ivs[d][:, hs])
        return carry

    lax.fori_loop(0, ncc, functools.partial(step, (qc_ref, ic_ref, (zfc_ref, zbc_ref)), ncc, False), 0)
    lax.fori_loop(0, nc, functools.partial(step, (q_ref, i_ref, (zf_ref, zb_ref)), nc, True), 0)

    def epilogue(i, carry):
        off = pl.multiple_of(i * L, L)
        for h in range(N_HEADS):
            hs = slice(h * HEAD_DIM, (h + 1) * HEAD_DIM)
            o = of_scr[pl.ds(off, L), hs] + ob_scr[pl.ds(off, L), hs]
            y = _rms(o, nw_ref[:, hs])
            g = g_ref[pl.ds(off, L), hs].astype(F32)
            o_ref[pl.ds(off, L), hs] = (y * (g * _sigmoid(g))).astype(o_ref.dtype)
        return carry

    lax.fori_loop(0, nc, epilogue, 0)


def _hgrn(p16_l, p32_l, p16_c, p32_c, lb, h_norm, batch, t, tc):
    w = W_MIX
    lat = lambda col: pl.BlockSpec((t, w), lambda b: (b, col))
    ctx = lambda col: pl.BlockSpec((tc, w), lambda b: (b, col))
    return pl.pallas_call(
        _hgrn_kernel,
        grid=(batch,),
        in_specs=[lat(P16_HQ), lat(P16_HI), lat(P16_HG), lat(0), lat(1),
                  ctx(P16_HQ), ctx(P16_HI), ctx(0), ctx(1),
                  pl.BlockSpec((2, w), lambda b: (0, 0)),
                  pl.BlockSpec((1, w), lambda b: (0, 0))],
        out_specs=pl.BlockSpec((t, w), lambda b: (b, 0)),
        out_shape=jax.ShapeDtypeStruct((batch * t, w), BF16),
        scratch_shapes=[pltpu.VMEM((2 * N_HEADS, HEAD_DIM, HEAD_DIM), F32),
                        pltpu.VMEM((t, w), F32), pltpu.VMEM((t, w), F32),
                        pltpu.VMEM((2, (2 + HGRN_CHUNK // HGRN_SUB) * HGRN_CHUNK, 4 * HGRN_CHUNK), BF16),
                        pltpu.VMEM((2, HGRN_SUB + 1, HGRN_CHUNK, HGRN_CHUNK), F32)],
        compiler_params=_cparams("parallel"),
        name="hgrn_scan",
    )(p16_l, p16_l, p16_l, p32_l, p32_l, p16_c, p16_c, p32_c, p32_c, lb, h_norm)


def _merge_kernel(a_ref, b_ref, ga_ref, gb_ref, x_ref, g1_ref, sh2_ref, sc2_ref, npost_ref, npre_ref,
                  wpa_ref, wpb_ref, wout_ref, wr_ref, br_ref,
                  x1_ref, h2_ref, te_ref, tw_ref, rk_ref, cnt_ref, run_scr):
    i = pl.program_id(0)
    tm = x_ref.shape[0]

    @pl.when(i == 0)
    def _():
        run_scr[...] = jnp.zeros_like(run_scr)

    ya = _dot(a_ref[...], wpa_ref[...])
    yb = _dot(b_ref[...], wpb_ref[...])
    y1 = _sigmoid(ga_ref[...].astype(F32)) * ya + _sigmoid(gb_ref[...].astype(F32)) * yb
    y = _dot(y1.astype(BF16), wout_ref[...])
    x1 = x_ref[...] + g1_ref[0] * _rms(y, npost_ref[...])
    x1_ref[...] = x1
    h2 = _rms(x1, npre_ref[...]) * (1.0 + sc2_ref[0]) + sh2_ref[0]
    h2_ref[...] = h2

    logits = _dot_f32(h2, wr_ref[...]) + br_ref[...]
    lane = _iota((tm, LANES), 1)
    vals, idxs = [], []
    lg = logits
    for _ in range(TOP_K):
        mx = jnp.max(lg, axis=1, keepdims=True)
        ix = jnp.min(jnp.where(lg == mx, lane, LANES), axis=1, keepdims=True)
        vals.append(mx)
        idxs.append(ix)
        lg = jnp.where(lane == ix, -jnp.inf, lg)
    es = [jnp.exp(v - vals[0]) for v in vals]
    den = es[0] + es[1] + es[2] + es[3]

    hits = [lane == ix for ix in idxs]
    cnt = sum(hh.astype(F32) for hh in hits)
    strict = (_iota((tm, tm), 1) < _iota((tm, tm), 0)).astype(BF16)
    base = _dot(strict, cnt.astype(BF16)) + run_scr[...]
    te = jnp.zeros((tm, LANES), I32)
    tw = jnp.zeros((tm, LANES), F32)
    rk = jnp.zeros((tm, LANES), I32)
    for k in range(TOP_K):
        rank = jnp.sum(jnp.where(hits[k], base, 0.0), axis=1, keepdims=True)
        te = jnp.where(lane == k, idxs[k], te)
        tw = jnp.where(lane == k, es[k] / den, tw)
        rk = jnp.where(lane == k, rank.astype(I32), rk)
    te_ref[...] = te
    tw_ref[...] = tw
    rk_ref[...] = rk
    run_scr[...] = run_scr[...] + jnp.sum(cnt, axis=0, keepdims=True)
    cnt_ref[...] = run_scr[...]


def _merge_route(a, b, p16, x2d, g1, sh2, sc2, npost, npre, wpa, wpb, wout, wr, br, t):
    m, d = x2d.shape
    tm = min(512, t)
    per = t // tm
    ga_blk = 1
    row = lambda width, col=0: pl.BlockSpec((tm, width), lambda i: (i, col))
    full = lambda r, c: pl.BlockSpec((r, c), lambda i: (0, 0))
    mod = pl.BlockSpec((1, 1, d), lambda i: (i // per, 0, 0))
    tok = lambda dt: jax.ShapeDtypeStruct((m, LANES), dt)
    return pl.pallas_call(
        _merge_kernel,
        grid=(m // tm,),
        in_specs=[row(W_MIX), row(W_MIX), row(d, ga_blk), row(d, ga_blk + 1), row(d),
                  mod, mod, mod, full(1, d), full(1, d),
                  full(W_MIX, d), full(W_MIX, d), full(d, d), full(d, LANES), full(1, LANES)],
        out_specs=[row(d), row(d), row(LANES), row(LANES), row(LANES), full(1, LANES)],
        out_shape=[jax.ShapeDtypeStruct((m, d), F32), jax.ShapeDtypeStruct((m, d), F32),
                   tok(I32), tok(F32), tok(I32), jax.ShapeDtypeStruct((1, LANES), F32)],
        scratch_shapes=[pltpu.VMEM((1, LANES), F32)],
        compiler_params=_cparams("arbitrary"),
        name="merge_route",
    )(a, b, p16, p16, x2d, g1, sh2, sc2, npost, npre, wpa, wpb, wout, wr, br)


def _dispatch_kernel(lo_ref, hi_ref, dest_ref, h_ref, xb_ref, sem, zsem, z_scr):
    i = pl.program_id(0)
    tm = h_ref.shape[0]

    def issue(r, carry):
        for k in range(TOP_K):
            dst = dest_ref[0, 0, r * TOP_K + k]
            pltpu.make_async_copy(h_ref.at[pl.ds(r, 1)], xb_ref.at[pl.ds(dst, 1)], sem).start()
        return carry

    lax.fori_loop(0, tm, issue, 0, unroll=8)

    @pl.when(i == 0)
    def _():
        z_scr[...] = jnp.zeros_like(z_scr)

        def pad_copy(r):
            return pltpu.make_async_copy(z_scr, xb_ref.at[pl.ds(r, 1)], zsem)

        for e in range(N_EXPERTS):
            lax.fori_loop(lo_ref[e], hi_ref[e], lambda r, carry: (pad_copy(r).start(), carry)[1], 0)
        for e in range(N_EXPERTS):
            lax.fori_loop(lo_ref[e], hi_ref[e], lambda r, carry: (pad_copy(r).wait(), carry)[1], 0)

    for k in range(TOP_K):
        pltpu.make_async_copy(h_ref, xb_ref.at[pl.ds(0, tm)], sem).wait()


def _dispatch(h2, dest, pad_lo, pad_hi, p_rows):
    m, d = h2.shape
    tm = 256
    nb = m // tm
    return pl.pallas_call(
        _dispatch_kernel,
        grid_spec=pltpu.PrefetchScalarGridSpec(
            num_scalar_prefetch=2,
            grid=(nb,),
            in_specs=[pl.BlockSpec((1, 1, tm * TOP_K), lambda i, lo, hi: (i, 0, 0), memory_space=pltpu.SMEM),
                      pl.BlockSpec((tm, d), lambda i, lo, hi: (i, 0))],
            out_specs=pl.BlockSpec(memory_space=pl.ANY),
            scratch_shapes=[pltpu.SemaphoreType.DMA, pltpu.SemaphoreType.DMA, pltpu.VMEM((1, d), F32)]),
        out_shape=jax.ShapeDtypeStruct((p_rows, d), F32),
        compiler_params=_cparams("arbitrary"),
        name="moe_dispatch",
    )(pad_lo, pad_hi, dest.reshape(nb, 1, tm * TOP_K), h2)


def _expert_kernel(be_ref, na_ref, x_ref, wgu_ref, bgu_ref, wdn_ref, bdn_ref, y_ref, wgu_scr, wdn_scr):
    i = pl.program_id(0)
    f = wdn_ref.shape[1]
    rows = 128

    @pl.when((i < na_ref[0]) & ((i == 0) | (be_ref[i] != be_ref[jnp.maximum(i - 1, 0)])))
    def _():
        def cast_gu(r, carry):
            sl = pl.ds(pl.multiple_of(r * rows, rows), rows)
            wgu_scr[sl, :] = wgu_ref[0, sl, :].astype(BF16)
            return carry

        def cast_dn(r, carry):
            sl = pl.ds(pl.multiple_of(r * rows, rows), rows)
            wdn_scr[sl, :] = wdn_ref[0, sl, :].astype(BF16)
            return carry

        lax.fori_loop(0, wgu_scr.shape[0] // rows, cast_gu, 0)
        lax.fori_loop(0, wdn_scr.shape[0] // rows, cast_dn, 0)

    @pl.when(i < na_ref[0])
    def _():
        gu = _dot(x_ref[...].astype(BF16), wgu_scr[...]) + bgu_ref[0]
        gate = jnp.minimum(gu[:, :f], SWIGLU_LIMIT)
        up = jnp.clip(gu[:, f:], -SWIGLU_LIMIT, SWIGLU_LIMIT)
        act = (up + 1.0) * gate * _sigmoid(SWIGLU_ALPHA * gate)
        y_ref[...] = _dot(act.astype(BF16), wdn_scr[...]) + bdn_ref[0]

    @pl.when(i >= na_ref[0])
    def _():
        y_ref[...] = jnp.zeros_like(y_ref)


def _experts(xb, block_e, n_active, wgu, bgu, wdn, bdn):
    p_rows, d = xb.shape
    e, _, f2 = wgu.shape
    f = f2 // 2
    blk = MOE_BLOCK
    nb = p_rows // blk
    return pl.pallas_call(
        _expert_kernel,
        grid_spec=pltpu.PrefetchScalarGridSpec(
            num_scalar_prefetch=2,
            grid=(nb,),
            in_specs=[pl.BlockSpec((blk, d), lambda i, be, na: (jnp.minimum(i, na[0] - 1), 0)),
                      pl.BlockSpec((1, d, f2), lambda i, be, na: (be[i], 0, 0)),
                      pl.BlockSpec((1, 1, f2), lambda i, be, na: (be[i], 0, 0)),
                      pl.BlockSpec((1, f, d), lambda i, be, na: (be[i], 0, 0)),
                      pl.BlockSpec((1, 1, d), lambda i, be, na: (be[i], 0, 0))],
            out_specs=pl.BlockSpec((blk, d), lambda i, be, na: (i, 0)),
            scratch_shapes=[pltpu.VMEM((d, f2), BF16), pltpu.VMEM((f, d), BF16)]),
        out_shape=jax.ShapeDtypeStruct((p_rows, d), F32),
        compiler_params=_cparams("arbitrary"),
        name="moe_experts",
    )(block_e, n_active, xb, wgu, bgu.reshape(e, 1, f2), wdn, bdn.reshape(e, 1, d))


def _combine_kernel(dest_ref, yb_ref, tw_ref, x1_ref, g2_ref, nw_ref, o_ref, buf, sem):
    tm = x1_ref.shape[0]

    def issue(r, carry):
        for k in range(TOP_K):
            src = dest_ref[0, 0, r * TOP_K + k]
            pltpu.make_async_copy(yb_ref.at[pl.ds(src, 1)], buf.at[k, pl.ds(r, 1)], sem).start()
        return carry

    lax.fori_loop(0, tm, issue, 0, unroll=8)
    for k in range(TOP_K):
        pltpu.make_async_copy(yb_ref.at[pl.ds(0, tm)], buf.at[k], sem).wait()
    tw = tw_ref[...]
    y = buf[0] * tw[:, 0:1]
    for k in range(1, TOP_K):
        y = y + buf[k] * tw[:, k:k + 1]
    o_ref[...] = x1_ref[...] + g2_ref[0] * _rms(y, nw_ref[...])


def _combine(yb, dest, tw, x1, g2, nw, t):
    m, d = x1.shape
    tm = 256
    per = t // tm
    nb = m // tm
    return pl.pallas_call(
        _combine_kernel,
        grid=(nb,),
        in_specs=[pl.BlockSpec((1, 1, tm * TOP_K), lambda i: (i, 0, 0), memory_space=pltpu.SMEM),
                  pl.BlockSpec(memory_space=pl.ANY),
                  pl.BlockSpec((tm, LANES), lambda i: (i, 0)),
                  pl.BlockSpec((tm, d), lambda i: (i, 0)),
                  pl.BlockSpec((1, 1, d), lambda i: (i // per, 0, 0)),
                  pl.BlockSpec((1, d), lambda i: (0, 0))],
        out_specs=pl.BlockSpec((tm, d), lambda i: (i, 0)),
        out_shape=jax.ShapeDtypeStruct((m, d), F32),
        scratch_shapes=[pltpu.VMEM((TOP_K, tm, d), F32), pltpu.SemaphoreType.DMA],
        compiler_params=_cparams("arbitrary"),
        name="moe_combine",
    )(dest.reshape(nb, 1, tm * TOP_K), yb, tw, x1, g2, nw)


def _gate_rows(p32, batch, t, L):
    g = p32[:, 2 * W_MIX:2 * W_MIX + 4 * N_HEADS].reshape(batch, t // L, L, 4 * N_HEADS)
    return jnp.swapaxes(g, 2, 3)


def kernel(x, c, ctx, c_ctx, w_ada, b_ada, norm_mix_pre, norm_mix_post, norm_ffn_pre, norm_ffn_post, w_in, b_in, conv_w, lb_raw, m_norm, h_norm, w_pa, w_pb, w_out, w_router, b_router, w_gu, b_gu, w_dn, b_dn):
    assert w_in.shape[0] == 1, "single-layer kernel: context tokens only hand over scan states"
    batch, t, d = x.shape
    tc = ctx.shape[1]
    w = W_MIX
    n_tok = batch * t

    rows = -(-(batch + 1) // 8) * 8
    c_all = jnp.zeros((rows, d), F32).at[:batch].set(c).at[batch].set(c_ctx)
    mod = _ada_mod(c_all, w_ada[0], b_ada[0])
    sh1, sc1, g1, sh2, sc2, g2 = [mod[:, i * d:(i + 1) * d] for i in range(6)]
    per_b = lambda a: a[:batch].reshape(batch, 1, d)
    per_c = lambda a: a[batch:batch + 1].reshape(1, 1, d)

    wi, bi = w_in[0], b_in[0]
    hf0, hf1 = 2 * w + 5 * w, 2 * w + 7 * w
    g0 = hf1 + 2 * d
    w16 = jnp.concatenate([wi[:, :2 * w], wi[:, hf1:g0], wi[:, 2 * w:hf0]], axis=1).astype(BF16)
    b16 = jnp.concatenate([bi[:2 * w], bi[hf1:g0], bi[2 * w:hf0]]).reshape(1, -1)
    n32 = 2 * w + LANES
    w32 = jnp.zeros((d, n32), F32).at[:, :2 * w].set(wi[:, hf0:hf1]).at[:, 2 * w:2 * w + 4 * N_HEADS].set(wi[:, g0:]).astype(BF16)
    b32 = jnp.zeros((n32,), F32).at[:2 * w].set(bi[hf0:hf1]).at[2 * w:2 * w + 4 * N_HEADS].set(bi[g0:]).reshape(1, -1)

    x2d = x.reshape(n_tok, d)
    c2d = ctx.reshape(batch * tc, d)
    nmp = norm_mix_pre[0].reshape(1, d)
    n16 = w16.shape[1]
    tn16 = n16 // 2
    p16_l = _proj(x2d, nmp, per_b(sc1), per_b(sh1), w16, b16, t, BF16, tn16)
    p32_l = _proj(x2d, nmp, per_b(sc1), per_b(sh1), w32, b32, t, F32, n32)
    p16_c = _proj(c2d, nmp, per_c(sc1), per_c(sh1), w16, b16, batch * tc, BF16, tn16)
    p32_c = _proj(c2d, nmp, per_c(sc1), per_c(sh1), w32, b32, batch * tc, F32, n32)

    cw9 = jnp.zeros((16, 2 * w), F32).at[:9].set(conv_w[0].reshape(9, 2 * w))
    qk_l = _conv_silu(p16_l, cw9, batch, t, GRID_W)
    qk_c = _conv_silu(p16_c, cw9, batch, tc, tc)

    a = _mlstm(qk_l, p16_l, p32_l, _gate_rows(p32_l, batch, t, MLSTM_CHUNK),
               qk_c, p16_c, p32_c, _gate_rows(p32_c, batch, tc, MLSTM_CHUNK),
               m_norm[0].reshape(1, w), batch, t, tc)

    lb_all = jnp.cumsum(jax.nn.softmax(lb_raw.astype(F32), axis=0), axis=0)
    b = _hgrn(p16_l, p32_l, p16_c, p32_c, lb_all[0], h_norm[0].reshape(1, w), batch, t, tc)

    wr = jnp.zeros((d, LANES), F32).at[:, :N_EXPERTS].set(w_router[0])
    br = jnp.full((1, LANES), NEG_BIG, F32).at[0, :N_EXPERTS].set(b_router[0])
    x1, h2, te, tw, rk, cnt = _merge_route(
        a, b, p16_l, x2d, per_b(g1), per_b(sh2), per_b(sc2),
        norm_mix_post[0].reshape(1, d), norm_ffn_pre[0].reshape(1, d),
        w_pa[0].astype(BF16), w_pb[0].astype(BF16), w_out[0].astype(BF16), wr, br, t)

    blk = MOE_BLOCK
    counts = cnt[0, :N_EXPERTS].astype(I32)
    pcounts = (counts + blk - 1) // blk * blk
    pend = jnp.cumsum(pcounts)
    pstart = pend - pcounts
    dest = (pstart[te[:, :TOP_K]] + rk[:, :TOP_K]).reshape(-1)
    p_rows = -(-n_tok * TOP_K // blk) * blk + N_EXPERTS * blk
    n_blocks = p_rows // blk
    block_row0 = jnp.arange(n_blocks, dtype=I32) * blk
    block_e = jnp.minimum(jnp.sum((pend[None, :] <= block_row0[:, None]).astype(I32), axis=1), N_EXPERTS - 1)
    n_active = (pend[-1:] // blk).astype(I32)

    xb = _dispatch(h2, dest, pstart + counts, pend, p_rows)
    yb = _experts(xb, block_e, n_active, w_gu[0], b_gu[0], w_dn[0], b_dn[0])
    out = _combine(yb, dest, tw, x1, per_b(g2), norm_ffn_post[0].reshape(1, d), t)
    return out.reshape(batch, t, d)
```

```python
import functools

import jax
import jax.numpy as jnp
from jax import lax
from jax.experimental import pallas as pl
from jax.experimental.pallas import tpu as pltpu

F32 = jnp.float32
BF16 = jnp.bfloat16
I32 = jnp.int32

EPS = 1e-6
N_HEADS = 4
HEAD_DIM = 128
W_MIX = N_HEADS * HEAD_DIM
GRID_W = 64
N_EXPERTS = 32
TOP_K = 4
SWIGLU_LIMIT = 7.0
SWIGLU_ALPHA = 1.702
LANES = 128
VMEM_LIMIT = 56 * 1024 * 1024

MLSTM_CHUNK = 128
HGRN_CHUNK = 64
HGRN_SUB = 8
MOE_BLOCK = 512
NEG_BIG = -1e30
P16_V, P16_MO, P16_HQ, P16_HI, P16_HG = 6, 7, 8, 9, 10


def _cparams(*sem):
    return pltpu.CompilerParams(dimension_semantics=sem, vmem_limit_bytes=VMEM_LIMIT)


def _dot(a, b):
    return jnp.dot(a, b, preferred_element_type=F32)


def _dot_nt(a, b):
    return lax.dot_general(a, b, (((1,), (1,)), ((), ())), preferred_element_type=F32)


def _dot_tn(a, b):
    return lax.dot_general(a, b, (((0,), (0,)), ((), ())), preferred_element_type=F32)


def _split3(x):
    x1 = x.astype(BF16)
    r1 = x - x1.astype(F32)
    x2 = r1.astype(BF16)
    r2 = r1 - x2.astype(F32)
    x3 = r2.astype(BF16)
    return x1, x2, x3


def _mask_dot(mask_bf16, x):
    x1, x2, x3 = _split3(x)
    return _dot(mask_bf16, x1) + _dot(mask_bf16, x2) + _dot(mask_bf16, x3)


def _dot_mask(x, mask_bf16):
    x1, x2, x3 = _split3(x)
    return _dot(x1, mask_bf16) + _dot(x2, mask_bf16) + _dot(x3, mask_bf16)


def _dot_f32(a, b):
    a1, a2, a3 = _split3(a)
    b1, b2, b3 = _split3(b)
    return (_dot(a1, b1) + (_dot(a1, b2) + _dot(a2, b1))
            + (_dot(a2, b2) + _dot(a1, b3) + _dot(a3, b1)))


def _sigmoid(x):
    return 1.0 / (1.0 + jnp.exp(-x))


def _log_sigmoid(x):
    return jnp.minimum(x, 0.0) - jnp.log(1.0 + jnp.exp(-jnp.abs(x)))


def _rms(xf, w):
    return xf * lax.rsqrt(jnp.mean(xf * xf, axis=-1, keepdims=True) + EPS) * w


def _iota(shape, dim):
    return lax.broadcasted_iota(I32, shape, dim)


def _ada_kernel(c_ref, w_ref, b_ref, o_ref):
    cv = c_ref[...]
    s = cv * _sigmoid(cv)
    o_ref[...] = _dot_f32(s, w_ref[...]) + b_ref[...]


def _ada_mod(c_all, w, b):
    rows, d = c_all.shape
    n = w.shape[1]
    tn = 1536
    return pl.pallas_call(
        _ada_kernel,
        grid=(n // tn,),
        in_specs=[pl.BlockSpec((rows, d), lambda j: (0, 0)),
                  pl.BlockSpec((d, tn), lambda j: (0, j)),
                  pl.BlockSpec((1, tn), lambda j: (0, j))],
        out_specs=pl.BlockSpec((rows, tn), lambda j: (0, j)),
        out_shape=jax.ShapeDtypeStruct((rows, n), F32),
        compiler_params=_cparams("parallel"),
        name="ada_mod",
    )(c_all, w, b.reshape(1, n))


def _proj_kernel(x_ref, nw_ref, sc_ref, sh_ref, w_ref, b_ref, o_ref, h_scr):
    @pl.when(pl.program_id(1) == 0)
    def _():
        h = _rms(x_ref[...], nw_ref[...]) * (1.0 + sc_ref[0]) + sh_ref[0]
        h_scr[...] = h.astype(BF16)

    o_ref[...] = (_dot(h_scr[...], w_ref[...]) + b_ref[...]).astype(o_ref.dtype)


def _proj(x2d, nw, sc, sh, w, b, tokens_per_mod, out_dtype, tn):
    m, d = x2d.shape
    n = w.shape[1]
    tm = min(1024, tokens_per_mod)
    per = tokens_per_mod // tm
    return pl.pallas_call(
        _proj_kernel,
        grid=(m // tm, n // tn),
        in_specs=[pl.BlockSpec((tm, d), lambda i, j: (i, 0)),
                  pl.BlockSpec((1, d), lambda i, j: (0, 0)),
                  pl.BlockSpec((1, 1, d), lambda i, j: (i // per, 0, 0)),
                  pl.BlockSpec((1, 1, d), lambda i, j: (i // per, 0, 0)),
                  pl.BlockSpec((d, tn), lambda i, j: (0, j)),
                  pl.BlockSpec((1, tn), lambda i, j: (0, j))],
        out_specs=pl.BlockSpec((tm, tn), lambda i, j: (i, j)),
        out_shape=jax.ShapeDtypeStruct((m, n), out_dtype),
        scratch_shapes=[pltpu.VMEM((tm, d), BF16)],
        compiler_params=_cparams("parallel", "arbitrary"),
        name="in_proj",
    )(x2d, nw, sc, sh, w, b)


def _conv_kernel(x_ref, w_ref, o_ref, *, cols, q_blocks):
    t, c = x_ref.shape
    x = x_ref[...].astype(F32)
    col = lax.rem(_iota((t, c), 0), cols)
    xm = jnp.where(col == 0, 0.0, pltpu.roll(x, 1, 0))
    xp = jnp.where(col == cols - 1, 0.0, pltpu.roll(x, t - 1, 0))

    def hrow(dr):
        return w_ref[3 * dr:3 * dr + 1, :] * xm + w_ref[3 * dr + 1:3 * dr + 2, :] * x + w_ref[3 * dr + 2:3 * dr + 3, :] * xp

    y = hrow(1)
    if t > cols:
        z = jnp.zeros((cols, c), F32)
        y = y + jnp.concatenate([z, hrow(0)[:t - cols]], axis=0) + jnp.concatenate([hrow(2)[cols:], z], axis=0)
    y = y * _sigmoid(y)
    scale = jnp.where(pl.program_id(1) < q_blocks, HEAD_DIM ** -0.5, 1.0)
    o_ref[...] = (y * scale).astype(o_ref.dtype)


def _conv_silu(p16, conv_w9, batch, t, cols):
    cw = 256
    nblk = 2 * W_MIX // cw
    return pl.pallas_call(
        functools.partial(_conv_kernel, cols=cols, q_blocks=W_MIX // cw),
        grid=(batch, nblk),
        in_specs=[pl.BlockSpec((t, cw), lambda b, j: (b, j)),
                  pl.BlockSpec((16, cw), lambda b, j: (0, j))],
        out_specs=pl.BlockSpec((t, cw), lambda b, j: (b, j)),
        out_shape=jax.ShapeDtypeStruct((batch * t, 2 * W_MIX), BF16),
        compiler_params=_cparams("parallel", "parallel"),
        name="conv_silu",
    )(p16, conv_w9)


def _mlstm_gate_sums(gcol, grow, rev):
    L = gcol.shape[0]
    r = _iota((L, L), 0)
    c = _iota((L, L), 1)
    lower = (c <= r)
    upper = (c >= r)
    m_col = (upper if rev else lower).astype(BF16)
    m_row = (lower if rev else upper).astype(BF16)
    lf_col = _log_sigmoid(gcol)
    lf_row = _log_sigmoid(grow)
    b_col = _mask_dot(m_col, lf_col)
    b_row = _dot_mask(lf_row, m_row)
    tot = jnp.sum(lf_row, axis=1, keepdims=True)
    causal = upper if rev else lower
    return b_col, b_row, tot, causal


def _mlstm_stage_state(q, k, v, gcol, grow, sums, x_state, m, h, d, with_out):
    b_col_all, b_row_all, tot_all, causal = sums
    L = q.shape[0]
    gi = d * N_HEADS + h
    gf = (2 + d) * N_HEADS + h
    assert L == HEAD_DIM
    ig_c = jnp.broadcast_to(gcol[:, gi:gi + 1], (L, HEAD_DIM))
    b_c = jnp.broadcast_to(b_col_all[:, gf:gf + 1], (L, HEAD_DIM))
    bl = jnp.broadcast_to(tot_all[gf:gf + 1, :], (1, HEAD_DIM))
    twice = lambda a: jnp.concatenate([a, a], axis=1)
    v1 = jnp.concatenate([v, jnp.ones((L, HEAD_DIM), BF16)], axis=1)
    part = None
    if with_out:
        logd = jnp.where(causal, b_c - b_row_all[gf:gf + 1, :] + grow[gi:gi + 1, :], -jnp.inf)
        m_inter = b_c + m
        m_t = jnp.maximum(jnp.max(logd, axis=1, keepdims=True), m_inter)
        part = dict(v1=v1, m_t=m_t, decay_mask=jnp.exp(logd - m_t), scores=_dot_nt(q, k),
                    inter=twice(jnp.exp(m_inter - m_t)) * _dot(q, x_state.astype(BF16)))
    logw = bl - b_c + ig_c
    m_new = jnp.maximum(bl + m, jnp.max(logw, axis=0, keepdims=True))
    w_c = jnp.exp(logw - m_new)
    decay = jnp.exp(bl + m - m_new)
    vw = (v1.astype(F32) * twice(w_c)).astype(BF16)
    x_new = twice(decay) * x_state + _dot_tn(k, vw)
    return part, x_new, m_new


def _mlstm_stage_out(part):
    s = part['scores'] * part['decay_mask']
    tot = _dot(s.astype(BF16), part['v1']) + part['inter']
    num = tot[:, :HEAD_DIM]
    den = tot[:, HEAD_DIM:]
    return num / jnp.maximum(jnp.abs(den), jnp.exp(-part['m_t']))


def _mlstm_kernel(q_ref, k_ref, v_ref, mo_ref, gc_ref, gr_ref,
                  qc_ref, kc_ref, vc_ref, gcc_ref, grc_ref, nw_ref,
                  o_ref, x_scr, hf_scr, hb_scr):
    L = MLSTM_CHUNK
    t = q_ref.shape[0]
    tc = qc_ref.shape[0]
    nc, ncc = t // L, tc // L
    x_scr[...] = jnp.zeros_like(x_scr)

    def step(refs, n_chunks, with_out, i, ms):
        qr, kr, vr, gcr, grr = refs
        new_ms, parts = [], []
        cis = [(n_chunks - 1 - i) if d else i for d in range(2)]
        offs = [pl.multiple_of(ci * L, L) for ci in cis]
        gcols = [gcr[pl.ds(offs[d], L), :] for d in range(2)]
        grows = [grr[0, cis[d]] for d in range(2)]
        sums = [_mlstm_gate_sums(gcols[d], grows[d], bool(d)) for d in range(2)]
        for d in range(2):
            for h in range(N_HEADS):
                hs = slice(h * HEAD_DIM, (h + 1) * HEAD_DIM)
                q = qr[pl.ds(offs[d], L), hs]
                k = kr[pl.ds(offs[d], L), hs]
                v = vr[pl.ds(offs[d], L), hs]
                idx = d * N_HEADS + h
                part, x_new, m_new = _mlstm_stage_state(q, k, v, gcols[d], grows[d], sums[d], x_scr[idx], ms[idx],
                                                        h, d, with_out)
                x_scr[idx] = x_new
                new_ms.append(m_new)
                parts.append(part)
        if with_out:
            for d in range(2):
                for h in range(N_HEADS):
                    hs = slice(h * HEAD_DIM, (h + 1) * HEAD_DIM)
                    (hb_scr if d else hf_scr)[pl.ds(offs[d], L), hs] = _mlstm_stage_out(parts[d * N_HEADS + h])
        return tuple(new_ms)

    ms0 = tuple(jnp.zeros((1, HEAD_DIM), F32) for _ in range(2 * N_HEADS))
    ctx_refs = (qc_ref, kc_ref, vc_ref, gcc_ref, grc_ref)
    lat_refs = (q_ref, k_ref, v_ref, gc_ref, gr_ref)
    ms = lax.fori_loop(0, ncc, functools.partial(step, ctx_refs, ncc, False), ms0)
    lax.fori_loop(0, nc, functools.partial(step, lat_refs, nc, True), ms)

    def epilogue(i, carry):
        off = pl.multiple_of(i * L, L)
        for h in range(N_HEADS):
            hs = slice(h * HEAD_DIM, (h + 1) * HEAD_DIM)
            o = hf_scr[pl.ds(off, L), hs] + hb_scr[pl.ds(off, L), hs]
            y = _rms(o, nw_ref[:, hs])
            o_ref[pl.ds(off, L), hs] = (y * _sigmoid(mo_ref[pl.ds(off, L), hs].astype(F32))).astype(o_ref.dtype)
        return carry

    lax.fori_loop(0, nc, epilogue, 0)


def _mlstm(qk_l, p16_l, p32_l, grow_l, qk_c, p16_c, p32_c, grow_c, m_norm, batch, t, tc):
    L = MLSTM_CHUNK
    w = W_MIX
    gcol_blk = p32_l.shape[1] // LANES - 1
    lat = lambda col: pl.BlockSpec((t, w), lambda b: (b, col))
    ctx = lambda col: pl.BlockSpec((tc, w), lambda b: (b, col))
    return pl.pallas_call(
        _mlstm_kernel,
        grid=(batch,),
        in_specs=[lat(0), lat(1), lat(P16_V), lat(P16_MO),
                  pl.BlockSpec((t, LANES), lambda b: (b, gcol_blk)),
                  pl.BlockSpec((1, t // L, 16, L), lambda b: (b, 0, 0, 0)),
                  ctx(0), ctx(1), ctx(P16_V),
                  pl.BlockSpec((tc, LANES), lambda b: (b, gcol_blk)),
                  pl.BlockSpec((1, tc // L, 16, L), lambda b: (b, 0, 0, 0)),
                  pl.BlockSpec((1, w), lambda b: (0, 0))],
        out_specs=pl.BlockSpec((t, w), lambda b: (b, 0)),
        out_shape=jax.ShapeDtypeStruct((batch * t, w), BF16),
        scratch_shapes=[pltpu.VMEM((2 * N_HEADS, HEAD_DIM, 2 * HEAD_DIM), F32),
                        pltpu.VMEM((t, w), F32), pltpu.VMEM((t, w), F32)],
        compiler_params=_cparams("parallel"),
        name="mlstm_scan",
    )(qk_l, qk_l, p16_l, p16_l, p32_l, grow_l, qk_c, qk_c, p16_c, p32_c, grow_c, m_norm)


def _hgrn_masks(L, c, rev):
    r = _iota((L, L), 0)
    u = _iota((L, L), 1)
    rb, ub = r // c, u // c
    n = L // c
    if not rev:
        mats = [u <= r,
                u > r,
                (ub == rb) & (u <= r),
                (ub == rb) & (u > r)]
        for i in range(2, n):
            mats.append((ub > rb) & (ub < i))
    else:
        mats = [u >= r, u < r, (ub == rb) & (u >= r), (ub == rb) & (u < r)]
        for i in range(0, n - 2):
            mats.append((ub < rb) & (ub > i))
    return jnp.concatenate([m.astype(BF16) for m in mats], axis=0)


def _hgrn_band_masks(L, c, rev):
    row = _iota((L, L), 0)
    lane = _iota((L, L), 1)
    dist = (lane - row) if rev else (row - lane)
    ms = [(dist == dl) for dl in range(c)]
    ms.append((dist >= c) & (lane // c != row // c))
    return jnp.stack([m.astype(F32) for m in ms], axis=0)


def _hgrn_stage_gates(z, lbv, qf, masks):
    L = HGRN_CHUNK
    e = jnp.exp(-jnp.abs(z))
    inv = 1.0 / (1.0 + e)
    sig = jnp.where(z >= 0, 1.0, e) * inv
    sig_n = jnp.where(z >= 0, e, 1.0) * inv
    f = lbv + (1.0 - lbv) * sig
    logf = jnp.log(f)
    kk = (1.0 - lbv) * sig_n
    qv = qf * _sigmoid(qf)
    x1, x2, x3 = _split3(logf)
    parts = jnp.concatenate([x1, x2, x3, jnp.zeros_like(x1)], axis=0)
    ex = jnp.exp(_dot(masks, parts))
    g = dict(f=f, kk=kk, qv=qv, ex=ex, decay=jnp.exp(jnp.sum(logf, axis=0, keepdims=True)),
             q_hat=(qv * ex[0:L]).astype(BF16), k_hat=(kk * ex[L:2 * L]).astype(BF16))
    if masks.shape[0] > 2 * L:
        g.update(q_t=(qv * ex[2 * L:3 * L]).astype(BF16), k_t=kk * ex[3 * L:4 * L])
    return g


def _hgrn_stage_cross(g, hs, rev):
    L, c = HGRN_CHUNK, HGRN_SUB
    n = L // c
    blocks = []
    for i in range(n):
        has_keys = (i >= 1) if not rev else (i <= n - 2)
        if not has_keys:
            blocks.append(jnp.zeros((c, L), F32))
            continue
        adjacent_only = (i == 1) if not rev else (i == n - 2)
        kf = g['k_t'][:, hs]
        if not adjacent_only:
            j = (4 + (i - 2)) if not rev else (4 + i)
            kf = kf * g['ex'][j * L:(j + 1) * L, hs]
        blocks.append(_dot_nt(g['q_t'][i * c:(i + 1) * c, hs], kf.astype(BF16)))
    return jnp.concatenate(blocks, axis=0)


def _hgrn_stage_band(g, rev):
    L, c = HGRN_CHUNK, HGRN_SUB
    p = g['kk']
    es = [(g['qv'] * p).astype(BF16)]
    for dl in range(1, c):
        p = g['f'] * pltpu.roll(p, (L - 1) if rev else 1, 0)
        es.append((g['qv'] * p).astype(BF16))
    return jnp.concatenate(es, axis=0)


def _hgrn_kernel(q_ref, i_ref, g_ref, zf_ref, zb_ref, qc_ref, ic_ref, zfc_ref, zbc_ref, lb_ref, nw_ref,
                 o_ref, st_scr, of_scr, ob_scr, mask_scr, band_scr):
    L = HGRN_CHUNK
    t = q_ref.shape[0]
    tc = qc_ref.shape[0]
    nc, ncc = t // L, tc // L
    st_scr[...] = jnp.zeros_like(st_scr)
    for d in range(2):
        m01 = _hgrn_masks(L, HGRN_SUB, bool(d))
        mask_scr[d] = jnp.concatenate([m01, m01, m01, jnp.zeros_like(m01)], axis=1)
        band_scr[d] = _hgrn_band_masks(L, HGRN_SUB, bool(d))

    def step(refs, n_chunks, with_out, i, carry):
        qr, ir, zrs = refs
        c = HGRN_SUB
        heads = [slice(h * HEAD_DIM, (h + 1) * HEAD_DIM) for h in range(N_HEADS)]
        offs = [pl.multiple_of(((n_chunks - 1 - i) if d else i) * L, L) for d in range(2)]
        gs = [_hgrn_stage_gates(zrs[d][pl.ds(offs[d], L), :], lb_ref[d:d + 1, :],
                                qr[pl.ds(offs[d], L), :].astype(F32),
                                mask_scr[d] if with_out else mask_scr[d, 0:2 * L]) for d in range(2)]
        ivs = [ir[pl.ds(offs[d], L), :] for d in range(2)]
        for d in range(2):
            for h, hs in enumerate(heads):
                idx = d * N_HEADS + h
                st = st_scr[idx]
                st_scr[idx] = gs[d]['decay'][:, hs] * st + _dot_tn(ivs[d][:, hs], gs[d]['k_hat'][:, hs])
                if with_out:
                    (ob_scr if d else of_scr)[pl.ds(offs[d], L), hs] = _dot_nt(gs[d]['q_hat'][:, hs], st.astype(BF16))
        if not with_out:
            return carry
        cross = [[_hgrn_stage_cross(gs[d], hs, bool(d)) for hs in heads] for d in range(2)]
        bands = [_hgrn_stage_band(gs[d], bool(d)) for d in range(2)]
        ones = jnp.ones((HEAD_DIM, L), BF16)
        rss = [[_dot(bands[d][:, hs], ones) for hs in heads] for d in range(2)]
        for d in range(2):
            for h, hs in enumerate(heads):
                s_mat = cross[d][h] * band_scr[d, c]
                for dl in range(c):
                    s_mat = s_mat + rss[d][h][dl * L:(dl + 1) * L] * band_scr[d, dl]
                o_scr = ob_scr if d else of_scr
                o_scr[pl.ds(offs[d], L), hs] = o_scr[pl.ds(offs[d], L), hs] + _dot(s_mat.astype(BF16), ivs[d][:, hs])
        return carry

    lax.fori_loop(0, ncc, functools.partial(step, (qc_ref, ic_ref, (zfc_ref, zbc_ref)), ncc, False), 0)
    lax.fori_loop(0, nc, functools.partial(step, (q_ref, i_ref, (zf_ref, zb_ref)), nc, True), 0)

    def epilogue(i, carry):
        off = pl.multiple_of(i * L, L)
        for h in range(N_HEADS):
            hs = slice(h * HEAD_DIM, (h + 1) * HEAD_DIM)
            o = of_scr[pl.ds(off, L), hs] + ob_scr[pl.ds(off, L), hs]
            y = _rms(o, nw_ref[:, hs])
            g = g_ref[pl.ds(off, L), hs].astype(F32)
            o_ref[pl.ds(off, L), hs] = (y * (g * _sigmoid(g))).astype(o_ref.dtype)
        return carry

    lax.fori_loop(0, nc, epilogue, 0)


def _hgrn(p16_l, p32_l, p16_c, p32_c, lb, h_norm, batch, t, tc):
    w = W_MIX
    lat = lambda col: pl.BlockSpec((t, w), lambda b: (b, col))
    ctx = lambda col: pl.BlockSpec((tc, w), lambda b: (b, col))
    return pl.pallas_call(
        _hgrn_kernel,
        grid=(batch,),
        in_specs=[lat(P16_HQ), lat(P16_HI), lat(P16_HG), lat(0), lat(1),
                  ctx(P16_HQ), ctx(P16_HI), ctx(0), ctx(1),
                  pl.BlockSpec((2, w), lambda b: (0, 0)),
                  pl.BlockSpec((1, w), lambda b: (0, 0))],
        out_specs=pl.BlockSpec((t, w), lambda b: (b, 0)),
        out_shape=jax.ShapeDtypeStruct((batch * t, w), BF16),
        scratch_shapes=[pltpu.VMEM((2 * N_HEADS, HEAD_DIM, HEAD_DIM), F32),
                        pltpu.VMEM((t, w), F32), pltpu.VMEM((t, w), F32),
                        pltpu.VMEM((2, (2 + HGRN_CHUNK // HGRN_SUB) * HGRN_CHUNK, 4 * HGRN_CHUNK), BF16),
                        pltpu.VMEM((2, HGRN_SUB + 1, HGRN_CHUNK, HGRN_CHUNK), F32)],
        compiler_params=_cparams("parallel"),
        name="hgrn_scan",
    )(p16_l, p16_l, p16_l, p32_l, p32_l, p16_c, p16_c, p32_c, p32_c, lb, h_norm)


def _merge_kernel(a_ref, b_ref, ga_ref, gb_ref, x_ref, g1_ref, sh2_ref, sc2_ref, npost_ref, npre_ref,
                  wpa_ref, wpb_ref, wout_ref, wr_ref, br_ref,
                  x1_ref, h2_ref, te_ref, tw_ref, rk_ref, cnt_ref, run_scr):
    i = pl.program_id(0)
    tm = x_ref.shape[0]

    @pl.when(i == 0)
    def _():
        run_scr[...] = jnp.zeros_like(run_scr)

    n_half = 2 if tm % 32 == 0 else 1
    th = tm // n_half
    halves = [pl.ds(j * th, th) for j in range(n_half)]
    yab = [(_dot(a_ref[s, :], wpa_ref[...]), _dot(b_ref[s, :], wpb_ref[...])) for s in halves]
    y1s = [(_sigmoid(ga_ref[s, :].astype(F32)) * ya + _sigmoid(gb_ref[s, :].astype(F32)) * yb).astype(BF16)
           for s, (ya, yb) in zip(halves, yab)]
    ys = [_dot(y1, wout_ref[...]) for y1 in y1s]
    h2s = []
    for s, y in zip(halves, ys):
        x1 = x_ref[s, :] + g1_ref[0] * _rms(y, npost_ref[...])
        x1_ref[s, :] = x1
        h2 = _rms(x1, npre_ref[...]) * (1.0 + sc2_ref[0]) + sh2_ref[0]
        h2_ref[s, :] = h2
        h2s.append(h2)
    logits = [_dot_f32(h2, wr_ref[...]) + br_ref[...] for h2 in h2s]

    lane = _iota((th, LANES), 1)
    tops = []
    for lg in logits:
        vals, idxs = [], []
        for _ in range(TOP_K):
            mx = jnp.max(lg, axis=1, keepdims=True)
            ix = jnp.min(jnp.where(lg == mx, lane, LANES), axis=1, keepdims=True)
            vals.append(mx)
            idxs.append(ix)
            lg = jnp.where(lane == ix, -jnp.inf, lg)
        es = [jnp.exp(v - vals[0]) for v in vals]
        hits = [lane == ix for ix in idxs]
        tops.append((idxs, es, es[0] + es[1] + es[2] + es[3], hits, sum(hh.astype(F32) for hh in hits)))

    cnt = jnp.concatenate([tp[4] for tp in tops], axis=0)
    strict = (_iota((tm, tm), 1) < _iota((tm, tm), 0)).astype(BF16)
    base_all = _dot(strict, cnt.astype(BF16)) + run_scr[...]
    for j, (s, (idxs, es, den, hits, _)) in enumerate(zip(halves, tops)):
        base = base_all[j * th:(j + 1) * th]
        te = jnp.zeros((th, LANES), I32)
        tw = jnp.zeros((th, LANES), F32)
        rk = jnp.zeros((th, LANES), I32)
        for k in range(TOP_K):
            rank = jnp.sum(jnp.where(hits[k], base, 0.0), axis=1, keepdims=True)
            te = jnp.where(lane == k, idxs[k], te)
            tw = jnp.where(lane == k, es[k] / den, tw)
            rk = jnp.where(lane == k, rank.astype(I32), rk)
        te_ref[s, :] = te
        tw_ref[s, :] = tw
        rk_ref[s, :] = rk
    run_scr[...] = run_scr[...] + jnp.sum(cnt, axis=0, keepdims=True)
    cnt_ref[...] = run_scr[...]


def _merge_route(a, b, p16, x2d, g1, sh2, sc2, npost, npre, wpa, wpb, wout, wr, br, t):
    m, d = x2d.shape
    tm = min(512, t)
    per = t // tm
    ga_blk = 1
    row = lambda width, col=0: pl.BlockSpec((tm, width), lambda i: (i, col))
    full = lambda r, c: pl.BlockSpec((r, c), lambda i: (0, 0))
    mod = pl.BlockSpec((1, 1, d), lambda i: (i // per, 0, 0))
    tok = lambda dt: jax.ShapeDtypeStruct((m, LANES), dt)
    return pl.pallas_call(
        _merge_kernel,
        grid=(m // tm,),
        in_specs=[row(W_MIX), row(W_MIX), row(d, ga_blk), row(d, ga_blk + 1), row(d),
                  mod, mod, mod, full(1, d), full(1, d),
                  full(W_MIX, d), full(W_MIX, d), full(d, d), full(d, LANES), full(1, LANES)],
        out_specs=[row(d), row(d), row(LANES), row(LANES), row(LANES), full(1, LANES)],
        out_shape=[jax.ShapeDtypeStruct((m, d), F32), jax.ShapeDtypeStruct((m, d), F32),
                   tok(I32), tok(F32), tok(I32), jax.ShapeDtypeStruct((1, LANES), F32)],
        scratch_shapes=[pltpu.VMEM((1, LANES), F32)],
        compiler_params=_cparams("arbitrary"),
        name="merge_route",
    )(a, b, p16, p16, x2d, g1, sh2, sc2, npost, npre, wpa, wpb, wout, wr, br)


def _dispatch_kernel(lo_ref, hi_ref, dest_ref, h_ref, xb_ref, sem, zsem, z_scr):
    i = pl.program_id(0)
    tm = h_ref.shape[0]

    def issue(r, carry):
        for k in range(TOP_K):
            dst = dest_ref[0, 0, r * TOP_K + k]
            pltpu.make_async_copy(h_ref.at[pl.ds(r, 1)], xb_ref.at[pl.ds(dst, 1)], sem).start()
        return carry

    lax.fori_loop(0, tm, issue, 0, unroll=8)

    @pl.when(i == 0)
    def _():
        z_scr[...] = jnp.zeros_like(z_scr)

        def pad_copy(r):
            return pltpu.make_async_copy(z_scr, xb_ref.at[pl.ds(r, 1)], zsem)

        for e in range(N_EXPERTS):
            lax.fori_loop(lo_ref[e], hi_ref[e], lambda r, carry: (pad_copy(r).start(), carry)[1], 0)
        for e in range(N_EXPERTS):
            lax.fori_loop(lo_ref[e], hi_ref[e], lambda r, carry: (pad_copy(r).wait(), carry)[1], 0)

    for k in range(TOP_K):
        pltpu.make_async_copy(h_ref, xb_ref.at[pl.ds(0, tm)], sem).wait()


def _dispatch(h2, dest, pad_lo, pad_hi, p_rows):
    m, d = h2.shape
    tm = 1024 if m % 1024 == 0 else 256
    nb = m // tm
    return pl.pallas_call(
        _dispatch_kernel,
        grid_spec=pltpu.PrefetchScalarGridSpec(
            num_scalar_prefetch=2,
            grid=(nb,),
            in_specs=[pl.BlockSpec((1, 1, tm * TOP_K), lambda i, lo, hi: (i, 0, 0), memory_space=pltpu.SMEM),
                      pl.BlockSpec((tm, d), lambda i, lo, hi: (i, 0))],
            out_specs=pl.BlockSpec(memory_space=pl.ANY),
            scratch_shapes=[pltpu.SemaphoreType.DMA, pltpu.SemaphoreType.DMA, pltpu.VMEM((1, d), F32)]),
        out_shape=jax.ShapeDtypeStruct((p_rows, d), F32),
        compiler_params=_cparams("arbitrary"),
        name="moe_dispatch",
    )(pad_lo, pad_hi, dest.reshape(nb, 1, tm * TOP_K), h2)


def _expert_kernel(be_ref, na_ref, x_ref, wgu_ref, bgu_ref, wdn_ref, bdn_ref, y_ref, wgu_scr, wdn_scr):
    i = pl.program_id(0)
    f = wdn_ref.shape[1]
    rows = 128

    @pl.when((i < na_ref[0]) & ((i == 0) | (be_ref[i] != be_ref[jnp.maximum(i - 1, 0)])))
    def _():
        def cast_gu(r, carry):
            sl = pl.ds(pl.multiple_of(r * rows, rows), rows)
            wgu_scr[sl, :] = wgu_ref[0, sl, :].astype(BF16)
            return carry

        def cast_dn(r, carry):
            sl = pl.ds(pl.multiple_of(r * rows, rows), rows)
            wdn_scr[sl, :] = wdn_ref[0, sl, :].astype(BF16)
            return carry

        lax.fori_loop(0, wgu_scr.shape[0] // rows, cast_gu, 0)
        lax.fori_loop(0, wdn_scr.shape[0] // rows, cast_dn, 0)

    @pl.when(i < na_ref[0])
    def _():
        gu = _dot(x_ref[...].astype(BF16), wgu_scr[...]) + bgu_ref[0]
        gate = jnp.minimum(gu[:, :f], SWIGLU_LIMIT)
        up = jnp.clip(gu[:, f:], -SWIGLU_LIMIT, SWIGLU_LIMIT)
        act = (up + 1.0) * gate * _sigmoid(SWIGLU_ALPHA * gate)
        y_ref[...] = _dot(act.astype(BF16), wdn_scr[...]) + bdn_ref[0]

    @pl.when(i >= na_ref[0])
    def _():
        y_ref[...] = jnp.zeros_like(y_ref)


def _experts(xb, block_e, n_active, wgu, bgu, wdn, bdn):
    p_rows, d = xb.shape
    e, _, f2 = wgu.shape
    f = f2 // 2
    blk = MOE_BLOCK
    nb = p_rows // blk
    return pl.pallas_call(
        _expert_kernel,
        grid_spec=pltpu.PrefetchScalarGridSpec(
            num_scalar_prefetch=2,
            grid=(nb,),
            in_specs=[pl.BlockSpec((blk, d), lambda i, be, na: (jnp.minimum(i, na[0] - 1), 0)),
                      pl.BlockSpec((1, d, f2), lambda i, be, na: (be[i], 0, 0)),
                      pl.BlockSpec((1, 1, f2), lambda i, be, na: (be[i], 0, 0)),
                      pl.BlockSpec((1, f, d), lambda i, be, na: (be[i], 0, 0)),
                      pl.BlockSpec((1, 1, d), lambda i, be, na: (be[i], 0, 0))],
            out_specs=pl.BlockSpec((blk, d), lambda i, be, na: (i, 0)),
            scratch_shapes=[pltpu.VMEM((d, f2), BF16), pltpu.VMEM((f, d), BF16)]),
        out_shape=jax.ShapeDtypeStruct((p_rows, d), F32),
        compiler_params=_cparams("arbitrary"),
        name="moe_experts",
    )(block_e, n_active, xb, wgu, bgu.reshape(e, 1, f2), wdn, bdn.reshape(e, 1, d))


def _combine_kernel(dest_ref, dest_next_ref, yb_ref, tw_ref, x1_ref, g2_ref, nw_ref, o_ref, buf, sems):
    i = pl.program_id(0)
    nb = pl.num_programs(0)
    tm = x1_ref.shape[0]
    slot = lax.rem(i, 2)

    def gather_tile(idx_ref, s):
        def issue(r, carry):
            for k in range(TOP_K):
                src = idx_ref[0, 0, r * TOP_K + k]
                pltpu.make_async_copy(yb_ref.at[pl.ds(src, 1)], buf.at[s, k, pl.ds(r, 1)], sems.at[s]).start()
            return carry

        lax.fori_loop(0, tm, issue, 0, unroll=8)

    @pl.when(i == 0)
    def _():
        gather_tile(dest_ref, 0)

    @pl.when(i + 1 < nb)
    def _():
        gather_tile(dest_next_ref, 1 - slot)

    for k in range(TOP_K):
        pltpu.make_async_copy(yb_ref.at[pl.ds(0, tm)], buf.at[slot, k], sems.at[slot]).wait()
    tw = tw_ref[...]
    y = buf[slot, 0] * tw[:, 0:1]
    for k in range(1, TOP_K):
        y = y + buf[slot, k] * tw[:, k:k + 1]
    o_ref[...] = x1_ref[...] + g2_ref[0] * _rms(y, nw_ref[...])


def _combine(yb, dest, tw, x1, g2, nw, t):
    m, d = x1.shape
    tm = 256
    per = t // tm
    nb = m // tm
    return pl.pallas_call(
        _combine_kernel,
        grid=(nb,),
        in_specs=[pl.BlockSpec((1, 1, tm * TOP_K), lambda i: (i, 0, 0), memory_space=pltpu.SMEM),
                  pl.BlockSpec((1, 1, tm * TOP_K), lambda i: (jnp.minimum(i + 1, nb - 1), 0, 0),
                               memory_space=pltpu.SMEM),
                  pl.BlockSpec(memory_space=pl.ANY),
                  pl.BlockSpec((tm, LANES), lambda i: (i, 0)),
                  pl.BlockSpec((tm, d), lambda i: (i, 0)),
                  pl.BlockSpec((1, 1, d), lambda i: (i // per, 0, 0)),
                  pl.BlockSpec((1, d), lambda i: (0, 0))],
        out_specs=pl.BlockSpec((tm, d), lambda i: (i, 0)),
        out_shape=jax.ShapeDtypeStruct((m, d), F32),
        scratch_shapes=[pltpu.VMEM((2, TOP_K, tm, d), F32), pltpu.SemaphoreType.DMA((2,))],
        compiler_params=_cparams("arbitrary"),
        name="moe_combine",
    )(dest.reshape(nb, 1, tm * TOP_K), dest.reshape(nb, 1, tm * TOP_K), yb, tw, x1, g2, nw)


def _gate_rows(p32, batch, t, L):
    g = p32[:, 2 * W_MIX:2 * W_MIX + 4 * N_HEADS].reshape(batch, t // L, L, 4 * N_HEADS)
    return jnp.swapaxes(g, 2, 3)


def kernel(x, c, ctx, c_ctx, w_ada, b_ada, norm_mix_pre, norm_mix_post, norm_ffn_pre, norm_ffn_post, w_in, b_in, conv_w, lb_raw, m_norm, h_norm, w_pa, w_pb, w_out, w_router, b_router, w_gu, b_gu, w_dn, b_dn):
    assert w_in.shape[0] == 1, "single-layer kernel: context tokens only hand over scan states"
    batch, t, d = x.shape
    tc = ctx.shape[1]
    w = W_MIX
    n_tok = batch * t

    rows = -(-(batch + 1) // 8) * 8
    c_all = jnp.zeros((rows, d), F32).at[:batch].set(c).at[batch].set(c_ctx)
    mod = _ada_mod(c_all, w_ada[0], b_ada[0])
    sh1, sc1, g1, sh2, sc2, g2 = [mod[:, i * d:(i + 1) * d] for i in range(6)]
    per_b = lambda a: a[:batch].reshape(batch, 1, d)
    per_c = lambda a: a[batch:batch + 1].reshape(1, 1, d)

    wi, bi = w_in[0], b_in[0]
    hf0, hf1 = 2 * w + 5 * w, 2 * w + 7 * w
    g0 = hf1 + 2 * d
    w16 = jnp.concatenate([wi[:, :2 * w], wi[:, hf1:g0], wi[:, 2 * w:hf0]], axis=1).astype(BF16)
    b16 = jnp.concatenate([bi[:2 * w], bi[hf1:g0], bi[2 * w:hf0]]).reshape(1, -1)
    n32 = 2 * w + LANES
    w32 = jnp.zeros((d, n32), F32).at[:, :2 * w].set(wi[:, hf0:hf1]).at[:, 2 * w:2 * w + 4 * N_HEADS].set(wi[:, g0:]).astype(BF16)
    b32 = jnp.zeros((n32,), F32).at[:2 * w].set(bi[hf0:hf1]).at[2 * w:2 * w + 4 * N_HEADS].set(bi[g0:]).reshape(1, -1)

    x2d = x.reshape(n_tok, d)
    c2d = ctx.reshape(batch * tc, d)
    nmp = norm_mix_pre[0].reshape(1, d)
    n16 = w16.shape[1]
    tn16 = n16 // 2
    p16_l = _proj(x2d, nmp, per_b(sc1), per_b(sh1), w16, b16, t, BF16, tn16)
    p32_l = _proj(x2d, nmp, per_b(sc1), per_b(sh1), w32, b32, t, F32, n32)
    p16_c = _proj(c2d, nmp, per_c(sc1), per_c(sh1), w16, b16, batch * tc, BF16, tn16)
    p32_c = _proj(c2d, nmp, per_c(sc1), per_c(sh1), w32, b32, batch * tc, F32, n32)

    cw9 = jnp.zeros((16, 2 * w), F32).at[:9].set(conv_w[0].reshape(9, 2 * w))
    qk_l = _conv_silu(p16_l, cw9, batch, t, GRID_W)
    qk_c = _conv_silu(p16_c, cw9, batch, tc, tc)

    a = _mlstm(qk_l, p16_l, p32_l, _gate_rows(p32_l, batch, t, MLSTM_CHUNK),
               qk_c, p16_c, p32_c, _gate_rows(p32_c, batch, tc, MLSTM_CHUNK),
               m_norm[0].reshape(1, w), batch, t, tc)

    lb_all = jnp.cumsum(jax.nn.softmax(lb_raw.astype(F32), axis=0), axis=0)
    b = _hgrn(p16_l, p32_l, p16_c, p32_c, lb_all[0], h_norm[0].reshape(1, w), batch, t, tc)

    wr = jnp.zeros((d, LANES), F32).at[:, :N_EXPERTS].set(w_router[0])
    br = jnp.full((1, LANES), NEG_BIG, F32).at[0, :N_EXPERTS].set(b_router[0])
    x1, h2, te, tw, rk, cnt = _merge_route(
        a, b, p16_l, x2d, per_b(g1), per_b(sh2), per_b(sc2),
        norm_mix_post[0].reshape(1, d), norm_ffn_pre[0].reshape(1, d),
        w_pa[0].astype(BF16), w_pb[0].astype(BF16), w_out[0].astype(BF16), wr, br, t)

    blk = MOE_BLOCK
    counts = cnt[0, :N_EXPERTS].astype(I32)
    pcounts = (counts + blk - 1) // blk * blk
    pend = jnp.cumsum(pcounts)
    pstart = pend - pcounts
    dest = (pstart[te[:, :TOP_K]] + rk[:, :TOP_K]).reshape(-1)
    p_rows = -(-n_tok * TOP_K // blk) * blk + N_EXPERTS * blk
    n_blocks = p_rows // blk
    block_row0 = jnp.arange(n_blocks, dtype=I32) * blk
    block_e = jnp.minimum(jnp.sum((pend[None, :] <= block_row0[:, None]).astype(I32), axis=1), N_EXPERTS - 1)
    n_active = (pend[-1:] // blk).astype(I32)

    xb = _dispatch(h2, dest, pstart + counts, pend, p_rows)
    yb = _experts(xb, block_e, n_active, w_gu[0], b_gu[0], w_dn[0], b_dn[0])
    out = _combine(yb, dest, tw, x1, per_b(g2), norm_ffn_post[0].reshape(1, d), t)
    return out.reshape(batch, t, d)
```

```python
import functools

import jax
import jax.numpy as jnp
from jax import lax
from jax.experimental import pallas as pl
from jax.experimental.pallas import tpu as pltpu

F32 = jnp.float32
BF16 = jnp.bfloat16
I32 = jnp.int32

EPS = 1e-6
N_HEADS = 4
HEAD_DIM = 128
W_MIX = N_HEADS * HEAD_DIM
GRID_W = 64
N_EXPERTS = 32
TOP_K = 4
SWIGLU_LIMIT = 7.0
SWIGLU_ALPHA = 1.702
LANES = 128
VMEM_LIMIT = 56 * 1024 * 1024

MLSTM_CHUNK = 128
HGRN_CHUNK = 64
HGRN_SUB = 8
MOE_BLOCK = 512
NEG_BIG = -1e30
P16_V, P16_MO, P16_HQ, P16_HI, P16_HG = 6, 7, 8, 9, 10


def _cparams(*sem):
    return pltpu.CompilerParams(dimension_semantics=sem, vmem_limit_bytes=VMEM_LIMIT)


def _dot(a, b):
    return jnp.dot(a, b, preferred_element_type=F32)


def _dot_nt(a, b):
    return lax.dot_general(a, b, (((1,), (1,)), ((), ())), preferred_element_type=F32)


def _dot_tn(a, b):
    return lax.dot_general(a, b, (((0,), (0,)), ((), ())), preferred_element_type=F32)


def _split3(x):
    x1 = x.astype(BF16)
    r1 = x - x1.astype(F32)
    x2 = r1.astype(BF16)
    r2 = r1 - x2.astype(F32)
    x3 = r2.astype(BF16)
    return x1, x2, x3


def _mask_dot(mask_bf16, x):
    x1, x2, x3 = _split3(x)
    return _dot(mask_bf16, x1) + _dot(mask_bf16, x2) + _dot(mask_bf16, x3)


def _dot_mask(x, mask_bf16):
    x1, x2, x3 = _split3(x)
    return _dot(x1, mask_bf16) + _dot(x2, mask_bf16) + _dot(x3, mask_bf16)


def _dot_f32(a, b):
    a1, a2, a3 = _split3(a)
    b1, b2, b3 = _split3(b)
    return (_dot(a1, b1) + (_dot(a1, b2) + _dot(a2, b1))
            + (_dot(a2, b2) + _dot(a1, b3) + _dot(a3, b1)))


def _sigmoid(x):
    return 1.0 / (1.0 + jnp.exp(-x))


def _log_sigmoid(x):
    return jnp.minimum(x, 0.0) - jnp.log(1.0 + jnp.exp(-jnp.abs(x)))


def _rms(xf, w):
    return xf * lax.rsqrt(jnp.mean(xf * xf, axis=-1, keepdims=True) + EPS) * w


def _iota(shape, dim):
    return lax.broadcasted_iota(I32, shape, dim)


def _ada_kernel(c_ref, w_ref, b_ref, o_ref):
    cv = c_ref[...]
    s = cv * _sigmoid(cv)
    o_ref[...] = _dot_f32(s, w_ref[...]) + b_ref[...]


def _ada_mod(c_all, w, b):
    rows, d = c_all.shape
    n = w.shape[1]
    tn = 1536
    return pl.pallas_call(
        _ada_kernel,
        grid=(n // tn,),
        in_specs=[pl.BlockSpec((rows, d), lambda j: (0, 0)),
                  pl.BlockSpec((d, tn), lambda j: (0, j)),
                  pl.BlockSpec((1, tn), lambda j: (0, j))],
        out_specs=pl.BlockSpec((rows, tn), lambda j: (0, j)),
        out_shape=jax.ShapeDtypeStruct((rows, n), F32),
        compiler_params=_cparams("parallel"),
        name="ada_mod",
    )(c_all, w, b.reshape(1, n))


def _proj_kernel(x_ref, nw_ref, sc_ref, sh_ref, w_ref, b_ref, o_ref, h_scr):
    @pl.when(pl.program_id(1) == 0)
    def _():
        h = _rms(x_ref[...], nw_ref[...]) * (1.0 + sc_ref[0]) + sh_ref[0]
        h_scr[...] = h.astype(BF16)

    o_ref[...] = (_dot(h_scr[...], w_ref[...]) + b_ref[...]).astype(o_ref.dtype)


def _proj(x2d, nw, sc, sh, w, b, tokens_per_mod, out_dtype, tn):
    m, d = x2d.shape
    n = w.shape[1]
    tm = min(1024, tokens_per_mod)
    per = tokens_per_mod // tm
    return pl.pallas_call(
        _proj_kernel,
        grid=(m // tm, n // tn),
        in_specs=[pl.BlockSpec((tm, d), lambda i, j: (i, 0)),
                  pl.BlockSpec((1, d), lambda i, j: (0, 0)),
                  pl.BlockSpec((1, 1, d), lambda i, j: (i // per, 0, 0)),
                  pl.BlockSpec((1, 1, d), lambda i, j: (i // per, 0, 0)),
                  pl.BlockSpec((d, tn), lambda i, j: (0, j)),
                  pl.BlockSpec((1, tn), lambda i, j: (0, j))],
        out_specs=pl.BlockSpec((tm, tn), lambda i, j: (i, j)),
        out_shape=jax.ShapeDtypeStruct((m, n), out_dtype),
        scratch_shapes=[pltpu.VMEM((tm, d), BF16)],
        compiler_params=_cparams("parallel", "arbitrary"),
        name="in_proj",
    )(x2d, nw, sc, sh, w, b)


def _conv_kernel(x_ref, w_ref, o_ref, *, cols, q_blocks):
    t, c = x_ref.shape
    x = x_ref[...].astype(F32)
    col = lax.rem(_iota((t, c), 0), cols)
    xm = jnp.where(col == 0, 0.0, pltpu.roll(x, 1, 0))
    xp = jnp.where(col == cols - 1, 0.0, pltpu.roll(x, t - 1, 0))

    def hrow(dr):
        return w_ref[3 * dr:3 * dr + 1, :] * xm + w_ref[3 * dr + 1:3 * dr + 2, :] * x + w_ref[3 * dr + 2:3 * dr + 3, :] * xp

    y = hrow(1)
    if t > cols:
        z = jnp.zeros((cols, c), F32)
        y = y + jnp.concatenate([z, hrow(0)[:t - cols]], axis=0) + jnp.concatenate([hrow(2)[cols:], z], axis=0)
    y = y * _sigmoid(y)
    scale = jnp.where(pl.program_id(1) < q_blocks, HEAD_DIM ** -0.5, 1.0)
    o_ref[...] = (y * scale).astype(o_ref.dtype)


def _conv_silu(p16, conv_w9, batch, t, cols):
    cw = 256
    nblk = 2 * W_MIX // cw
    return pl.pallas_call(
        functools.partial(_conv_kernel, cols=cols, q_blocks=W_MIX // cw),
        grid=(batch, nblk),
        in_specs=[pl.BlockSpec((t, cw), lambda b, j: (b, j)),
                  pl.BlockSpec((16, cw), lambda b, j: (0, j))],
        out_specs=pl.BlockSpec((t, cw), lambda b, j: (b, j)),
        out_shape=jax.ShapeDtypeStruct((batch * t, 2 * W_MIX), BF16),
        compiler_params=_cparams("parallel", "parallel"),
        name="conv_silu",
    )(p16, conv_w9)


def _mlstm_gate_sums(gcol, grow, rev):
    L = gcol.shape[0]
    r = _iota((L, L), 0)
    c = _iota((L, L), 1)
    lower = (c <= r)
    upper = (c >= r)
    m_col = (upper if rev else lower).astype(BF16)
    m_row = (lower if rev else upper).astype(BF16)
    lf_col = _log_sigmoid(gcol)
    lf_row = _log_sigmoid(grow)
    b_col = _mask_dot(m_col, lf_col)
    b_row = _dot_mask(lf_row, m_row)
    tot = jnp.sum(lf_row, axis=1, keepdims=True)
    causal = upper if rev else lower
    return b_col, b_row, tot, causal


def _mlstm_stage_state(q, k, v, gcol, grow, sums, x_state, m, h, d, with_out):
    b_col_all, b_row_all, tot_all, causal = sums
    L = q.shape[0]
    gi = d * N_HEADS + h
    gf = (2 + d) * N_HEADS + h
    assert L == HEAD_DIM
    ig_c = jnp.broadcast_to(gcol[:, gi:gi + 1], (L, HEAD_DIM))
    b_c = jnp.broadcast_to(b_col_all[:, gf:gf + 1], (L, HEAD_DIM))
    bl = jnp.broadcast_to(tot_all[gf:gf + 1, :], (1, HEAD_DIM))
    twice = lambda a: jnp.concatenate([a, a], axis=1)
    v1 = jnp.concatenate([v, jnp.ones((L, HEAD_DIM), BF16)], axis=1)
    part = None
    if with_out:
        logd = jnp.where(causal, b_c - b_row_all[gf:gf + 1, :] + grow[gi:gi + 1, :], -jnp.inf)
        m_inter = b_c + m
        m_t = jnp.maximum(jnp.max(logd, axis=1, keepdims=True), m_inter)
        part = dict(v1=v1, m_t=m_t, decay_mask=jnp.exp(logd - m_t), scores=_dot_nt(q, k),
                    inter=twice(jnp.exp(m_inter - m_t)) * _dot(q, x_state.astype(BF16)))
    logw = bl - b_c + ig_c
    m_new = jnp.maximum(bl + m, jnp.max(logw, axis=0, keepdims=True))
    w_c = jnp.exp(logw - m_new)
    decay = jnp.exp(bl + m - m_new)
    vw = (v1.astype(F32) * twice(w_c)).astype(BF16)
    x_new = twice(decay) * x_state + _dot_tn(k, vw)
    return part, x_new, m_new


def _mlstm_stage_out(part):
    s = part['scores'] * part['decay_mask']
    tot = _dot(s.astype(BF16), part['v1']) + part['inter']
    num = tot[:, :HEAD_DIM]
    den = tot[:, HEAD_DIM:]
    return num / jnp.maximum(jnp.abs(den), jnp.exp(-part['m_t']))


def _mlstm_kernel(q_ref, k_ref, v_ref, mo_ref, gc_ref, gr_ref,
                  qc_ref, kc_ref, vc_ref, gcc_ref, grc_ref, nw_ref,
                  o_ref, x_scr, hf_scr, hb_scr):
    L = MLSTM_CHUNK
    t = q_ref.shape[0]
    tc = qc_ref.shape[0]
    nc, ncc = t // L, tc // L
    x_scr[...] = jnp.zeros_like(x_scr)

    def step(refs, n_chunks, with_out, i, ms):
        qr, kr, vr, gcr, grr = refs
        new_ms, parts = [], []
        cis = [(n_chunks - 1 - i) if d else i for d in range(2)]
        offs = [pl.multiple_of(ci * L, L) for ci in cis]
        gcols = [gcr[pl.ds(offs[d], L), :] for d in range(2)]
        grows = [grr[0, cis[d]] for d in range(2)]
        sums = [_mlstm_gate_sums(gcols[d], grows[d], bool(d)) for d in range(2)]
        for d in range(2):
            for h in range(N_HEADS):
                hs = slice(h * HEAD_DIM, (h + 1) * HEAD_DIM)
                q = qr[pl.ds(offs[d], L), hs]
                k = kr[pl.ds(offs[d], L), hs]
                v = vr[pl.ds(offs[d], L), hs]
                idx = d * N_HEADS + h
                part, x_new, m_new = _mlstm_stage_state(q, k, v, gcols[d], grows[d], sums[d], x_scr[idx], ms[idx],
                                                        h, d, with_out)
                x_scr[idx] = x_new
                new_ms.append(m_new)
                parts.append(part)
        if with_out:
            for d in range(2):
                for h in range(N_HEADS):
                    hs = slice(h * HEAD_DIM, (h + 1) * HEAD_DIM)
                    (hb_scr if d else hf_scr)[pl.ds(offs[d], L), hs] = _mlstm_stage_out(parts[d * N_HEADS + h])
        return tuple(new_ms)

    ms0 = tuple(jnp.zeros((1, HEAD_DIM), F32) for _ in range(2 * N_HEADS))
    ctx_refs = (qc_ref, kc_ref, vc_ref, gcc_ref, grc_ref)
    lat_refs = (q_ref, k_ref, v_ref, gc_ref, gr_ref)
    ms = lax.fori_loop(0, ncc, functools.partial(step, ctx_refs, ncc, False), ms0)
    lax.fori_loop(0, nc, functools.partial(step, lat_refs, nc, True), ms)

    def epilogue(i, carry):
        off = pl.multiple_of(i * L, L)
        for h in range(N_HEADS):
            hs = slice(h * HEAD_DIM, (h + 1) * HEAD_DIM)
            o = hf_scr[pl.ds(off, L), hs] + hb_scr[pl.ds(off, L), hs]
            y = _rms(o, nw_ref[:, hs])
            o_ref[pl.ds(off, L), hs] = (y * _sigmoid(mo_ref[pl.ds(off, L), hs].astype(F32))).astype(o_ref.dtype)
        return carry

    lax.fori_loop(0, nc, epilogue, 0)


def _mlstm(qk_l, p16_l, p32_l, grow_l, qk_c, p16_c, p32_c, grow_c, m_norm, batch, t, tc):
    L = MLSTM_CHUNK
    w = W_MIX
    gcol_blk = p32_l.shape[1] // LANES - 1
    lat = lambda col: pl.BlockSpec((t, w), lambda b: (b, col))
    ctx = lambda col: pl.BlockSpec((tc, w), lambda b: (b, col))
    return pl.pallas_call(
        _mlstm_kernel,
        grid=(batch,),
        in_specs=[lat(0), lat(1), lat(P16_V), lat(P16_MO),
                  pl.BlockSpec((t, LANES), lambda b: (b, gcol_blk)),
                  pl.BlockSpec((1, t // L, 16, L), lambda b: (b, 0, 0, 0)),
                  ctx(0), ctx(1), ctx(P16_V),
                  pl.BlockSpec((tc, LANES), lambda b: (b, gcol_blk)),
                  pl.BlockSpec((1, tc // L, 16, L), lambda b: (b, 0, 0, 0)),
                  pl.BlockSpec((1, w), lambda b: (0, 0))],
        out_specs=pl.BlockSpec((t, w), lambda b: (b, 0)),
        out_shape=jax.ShapeDtypeStruct((batch * t, w), BF16),
        scratch_shapes=[pltpu.VMEM((2 * N_HEADS, HEAD_DIM, 2 * HEAD_DIM), F32),
                        pltpu.VMEM((t, w), F32), pltpu.VMEM((t, w), F32)],
        compiler_params=_cparams("parallel"),
        name="mlstm_scan",
    )(qk_l, qk_l, p16_l, p16_l, p32_l, grow_l, qk_c, qk_c, p16_c, p32_c, grow_c, m_norm)


def _hgrn_masks(L, c, rev):
    r = _iota((L, L), 0)
    u = _iota((L, L), 1)
    rb, ub = r // c, u // c
    n = L // c
    if not rev:
        mats = [u <= r,
                u > r,
                (ub == rb) & (u <= r),
                (ub == rb) & (u > r)]
        for i in range(2, n):
            mats.append((ub > rb) & (ub < i))
    else:
        mats = [u >= r, u < r, (ub == rb) & (u >= r), (ub == rb) & (u < r)]
        for i in range(0, n - 2):
            mats.append((ub < rb) & (ub > i))
    return jnp.concatenate([m.astype(BF16) for m in mats], axis=0)


def _hgrn_band_masks(L, c, rev):
    row = _iota((L, L), 0)
    lane = _iota((L, L), 1)
    dist = (lane - row) if rev else (row - lane)
    ms = [(dist == dl) for dl in range(c)]
    ms.append((dist >= c) & (lane // c != row // c))
    return jnp.stack([m.astype(F32) for m in ms], axis=0)


def _hgrn_stage_gates(z, lbv, qf, masks):
    L = HGRN_CHUNK
    e = jnp.exp(-jnp.abs(z))
    inv = 1.0 / (1.0 + e)
    sig = jnp.where(z >= 0, 1.0, e) * inv
    sig_n = jnp.where(z >= 0, e, 1.0) * inv
    f = lbv + (1.0 - lbv) * sig
    logf = jnp.log(f)
    kk = (1.0 - lbv) * sig_n
    qv = qf * _sigmoid(qf)
    x1, x2, x3 = _split3(logf)
    parts = jnp.concatenate([x1, x2, x3, jnp.zeros_like(x1)], axis=0)
    ex = jnp.exp(_dot(masks, parts))
    g = dict(f=f, kk=kk, qv=qv, ex=ex, decay=jnp.exp(jnp.sum(logf, axis=0, keepdims=True)),
             q_hat=(qv * ex[0:L]).astype(BF16), k_hat=(kk * ex[L:2 * L]).astype(BF16))
    if masks.shape[0] > 2 * L:
        g.update(q_t=(qv * ex[2 * L:3 * L]).astype(BF16), k_t=kk * ex[3 * L:4 * L])
    return g


def _hgrn_stage_cross(g, hs, rev):
    L, c = HGRN_CHUNK, HGRN_SUB
    n = L // c
    blocks = []
    for i in range(n):
        has_keys = (i >= 1) if not rev else (i <= n - 2)
        if not has_keys:
            blocks.append(jnp.zeros((c, L), F32))
            continue
        adjacent_only = (i == 1) if not rev else (i == n - 2)
        kf = g['k_t'][:, hs]
        if not adjacent_only:
            j = (4 + (i - 2)) if not rev else (4 + i)
            kf = kf * g['ex'][j * L:(j + 1) * L, hs]
        blocks.append(_dot_nt(g['q_t'][i * c:(i + 1) * c, hs], kf.astype(BF16)))
    return jnp.concatenate(blocks, axis=0)


def _hgrn_stage_band(g, rev):
    L, c = HGRN_CHUNK, HGRN_SUB
    p = g['kk']
    es = [(g['qv'] * p).astype(BF16)]
    for dl in range(1, c):
        p = g['f'] * pltpu.roll(p, (L - 1) if rev else 1, 0)
        es.append((g['qv'] * p).astype(BF16))
    return jnp.concatenate(es, axis=0)


def _hgrn_kernel(q_ref, i_ref, g_ref, zf_ref, zb_ref, qc_ref, ic_ref, zfc_ref, zbc_ref, lb_ref, nw_ref,
                 o_ref, st_scr, of_scr, ob_scr, mask_scr, band_scr):
    L = HGRN_CHUNK
    t = q_ref.shape[0]
    tc = qc_ref.shape[0]
    nc, ncc = t // L, tc // L
    st_scr[...] = jnp.zeros_like(st_scr)
    for d in range(2):
        m01 = _hgrn_masks(L, HGRN_SUB, bool(d))
        mask_scr[d] = jnp.concatenate([m01, m01, m01, jnp.zeros_like(m01)], axis=1)
        band_scr[d] = _hgrn_band_masks(L, HGRN_SUB, bool(d))

    def step(refs, n_chunks, with_out, i, carry):
        qr, ir, zrs = refs
        c = HGRN_SUB
        heads = [slice(h * HEAD_DIM, (h + 1) * HEAD_DIM) for h in range(N_HEADS)]
        offs = [pl.multiple_of(((n_chunks - 1 - i) if d else i) * L, L) for d in range(2)]
        gs = [_hgrn_stage_gates(zrs[d][pl.ds(offs[d], L), :], lb_ref[d:d + 1, :],
                                qr[pl.ds(offs[d], L), :].astype(F32),
                                mask_scr[d] if with_out else mask_scr[d, 0:2 * L]) for d in range(2)]
        ivs = [ir[pl.ds(offs[d], L), :] for d in range(2)]
        for d in range(2):
            for h, hs in enumerate(heads):
                idx = d * N_HEADS + h
                st = st_scr[idx]
                st_scr[idx] = gs[d]['decay'][:, hs] * st + _dot_tn(ivs[d][:, hs], gs[d]['k_hat'][:, hs])
                if with_out:
                    (ob_scr if d else of_scr)[pl.ds(offs[d], L), hs] = _dot_nt(gs[d]['q_hat'][:, hs], st.astype(BF16))
        if not with_out:
            return carry
        cross = [[_hgrn_stage_cross(gs[d], hs, bool(d)) for hs in heads] for d in range(2)]
        bands = [_hgrn_stage_band(gs[d], bool(d)) for d in range(2)]
        ones = jnp.ones((HEAD_DIM, L), BF16)
        rss = [[_dot(bands[d][:, hs], ones) for hs in heads] for d in range(2)]
        for d in range(2):
            for h, hs in enumerate(heads):
                s_mat = cross[d][h] * band_scr[d, c]
                for dl in range(c):
                    s_mat = s_mat + rss[d][h][dl * L:(dl + 1) * L] * band_scr[d, dl]
                o_scr = ob_scr if d else of_scr
                o_scr[pl.ds(offs[d], L), hs] = o_scr[pl.ds(offs[d], L), hs] + _dot(s_mat.astype(BF16), ivs[d][:, hs])
        return carry

    lax.fori_loop(0, ncc, functools.partial(step, (qc_ref, ic_ref, (zfc_ref, zbc_ref)), ncc, False), 0)
    lax.fori_loop(0, nc, functools.partial(step, (q_ref, i_ref, (zf_ref, zb_ref)), nc, True), 0)

    def epilogue(i, carry):
        off = pl.multiple_of(i * L, L)
        for h in range(N_HEADS):
            hs = slice(h * HEAD_DIM, (h + 1) * HEAD_DIM)
            o = of_scr[pl.ds(off, L), hs] + ob_scr[pl.ds(off, L), hs]
            y = _rms(o, nw_ref[:, hs])
            g = g_ref[pl.ds(off, L), hs].astype(F32)
            o_ref[pl.ds(off, L), hs] = (y * (g * _sigmoid(g))).astype(o_ref.dtype)
        return carry

    lax.fori_loop(0, nc, epilogue, 0)


def _hgrn(p16_l, p32_l, p16_c, p32_c, lb, h_norm, batch, t, tc):
    w = W_MIX
    lat = lambda col: pl.BlockSpec((t, w), lambda b: (b, col))
    ctx = lambda col: pl.BlockSpec((tc, w), lambda b: (b, col))
    return pl.pallas_call(
        _hgrn_kernel,
        grid=(batch,),
        in_specs=[lat(P16_HQ), lat(P16_HI), lat(P16_HG), lat(0), lat(1),
                  ctx(P16_HQ), ctx(P16_HI), ctx(0), ctx(1),
                  pl.BlockSpec((2, w), lambda b: (0, 0)),
                  pl.BlockSpec((1, w), lambda b: (0, 0))],
        out_specs=pl.BlockSpec((t, w), lambda b: (b, 0)),
        out_shape=jax.ShapeDtypeStruct((batch * t, w), BF16),
        scratch_shapes=[pltpu.VMEM((2 * N_HEADS, HEAD_DIM, HEAD_DIM), F32),
                        pltpu.VMEM((t, w), F32), pltpu.VMEM((t, w), F32),
                        pltpu.VMEM((2, (2 + HGRN_CHUNK // HGRN_SUB) * HGRN_CHUNK, 4 * HGRN_CHUNK), BF16),
                        pltpu.VMEM((2, HGRN_SUB + 1, HGRN_CHUNK, HGRN_CHUNK), F32)],
        compiler_params=_cparams("parallel"),
        name="hgrn_scan",
    )(p16_l, p16_l, p16_l, p32_l, p32_l, p16_c, p16_c, p32_c, p32_c, lb, h_norm)


def _merge_kernel(a_ref, b_ref, ga_ref, gb_ref, x_ref, g1_ref, sh2_ref, sc2_ref, npost_ref, npre_ref,
                  wpa_ref, wpb_ref, wout_ref, wr_ref, br_ref,
                  x1_ref, h2_ref, te_ref, tw_ref, rk_ref, cnt_ref, run_scr):
    i = pl.program_id(0)
    tm = x_ref.shape[0]

    @pl.when(i == 0)
    def _():
        run_scr[...] = jnp.zeros_like(run_scr)

    n_half = 2 if tm % 32 == 0 else 1
    th = tm // n_half
    halves = [pl.ds(j * th, th) for j in range(n_half)]
    yab = [(_dot(a_ref[s, :], wpa_ref[...]), _dot(b_ref[s, :], wpb_ref[...])) for s in halves]
    y1s = [(_sigmoid(ga_ref[s, :].astype(F32)) * ya + _sigmoid(gb_ref[s, :].astype(F32)) * yb).astype(BF16)
           for s, (ya, yb) in zip(halves, yab)]
    ys = [_dot(y1, wout_ref[...]) for y1 in y1s]
    h2s = []
    for s, y in zip(halves, ys):
        x1 = x_ref[s, :] + g1_ref[0] * _rms(y, npost_ref[...])
        x1_ref[s, :] = x1
        h2 = _rms(x1, npre_ref[...]) * (1.0 + sc2_ref[0]) + sh2_ref[0]
        h2_ref[s, :] = h2
        h2s.append(h2)
    logits = [_dot_f32(h2, wr_ref[...]) + br_ref[...] for h2 in h2s]

    lane = _iota((th, LANES), 1)
    tops = []
    for lg in logits:
        vals, idxs = [], []
        for _ in range(TOP_K):
            mx = jnp.max(lg, axis=1, keepdims=True)
            ix = jnp.min(jnp.where(lg == mx, lane, LANES), axis=1, keepdims=True)
            vals.append(mx)
            idxs.append(ix)
            lg = jnp.where(lane == ix, -jnp.inf, lg)
        es = [jnp.exp(v - vals[0]) for v in vals]
        hits = [lane == ix for ix in idxs]
        tops.append((idxs, es, es[0] + es[1] + es[2] + es[3], hits, sum(hh.astype(F32) for hh in hits)))

    cnt = jnp.concatenate([tp[4] for tp in tops], axis=0)
    strict = (_iota((tm, tm), 1) < _iota((tm, tm), 0)).astype(BF16)
    base_all = _dot(strict, cnt.astype(BF16)) + run_scr[...]
    for j, (s, (idxs, es, den, hits, _)) in enumerate(zip(halves, tops)):
        base = base_all[j * th:(j + 1) * th]
        te = jnp.zeros((th, LANES), I32)
        tw = jnp.zeros((th, LANES), F32)
        rk = jnp.zeros((th, LANES), I32)
        for k in range(TOP_K):
            rank = jnp.sum(jnp.where(hits[k], base, 0.0), axis=1, keepdims=True)
            te = jnp.where(lane == k, idxs[k], te)
            tw = jnp.where(lane == k, es[k] / den, tw)
            rk = jnp.where(lane == k, rank.astype(I32), rk)
        te_ref[s, :] = te
        tw_ref[s, :] = tw
        rk_ref[s, :] = rk
    run_scr[...] = run_scr[...] + jnp.sum(cnt, axis=0, keepdims=True)
    cnt_ref[...] = run_scr[...]


def _merge_route(a, b, p16, x2d, g1, sh2, sc2, npost, npre, wpa, wpb, wout, wr, br, t):
    m, d = x2d.shape
    tm = min(512, t)
    per = t // tm
    ga_blk = 1
    row = lambda width, col=0: pl.BlockSpec((tm, width), lambda i: (i, col))
    full = lambda r, c: pl.BlockSpec((r, c), lambda i: (0, 0))
    mod = pl.BlockSpec((1, 1, d), lambda i: (i // per, 0, 0))
    tok = lambda dt: jax.ShapeDtypeStruct((m, LANES), dt)
    return pl.pallas_call(
        _merge_kernel,
        grid=(m // tm,),
        in_specs=[row(W_MIX), row(W_MIX), row(d, ga_blk), row(d, ga_blk + 1), row(d),
                  mod, mod, mod, full(1, d), full(1, d),
                  full(W_MIX, d), full(W_MIX, d), full(d, d), full(d, LANES), full(1, LANES)],
        out_specs=[row(d), row(d), row(LANES), row(LANES), row(LANES), full(1, LANES)],
        out_shape=[jax.ShapeDtypeStruct((m, d), F32), jax.ShapeDtypeStruct((m, d), F32),
                   tok(I32), tok(F32), tok(I32), jax.ShapeDtypeStruct((1, LANES), F32)],
        scratch_shapes=[pltpu.VMEM((1, LANES), F32)],
        compiler_params=_cparams("arbitrary"),
        name="merge_route",
    )(a, b, p16, p16, x2d, g1, sh2, sc2, npost, npre, wpa, wpb, wout, wr, br)


def _dispatch_kernel(lo_ref, hi_ref, dest_ref, h_ref, xb_ref, sem, zsem, z_scr):
    i = pl.program_id(0)
    tm = h_ref.shape[0]

    def issue(r, carry):
        for k in range(TOP_K):
            dst = dest_ref[0, 0, r * TOP_K + k]
            pltpu.make_async_copy(h_ref.at[pl.ds(r, 1)], xb_ref.at[pl.ds(dst, 1)], sem).start(priority=k % 2)
        return carry

    lax.fori_loop(0, tm, issue, 0, unroll=8)

    @pl.when(i == 0)
    def _():
        z_scr[...] = jnp.zeros_like(z_scr)

        def pad_copy(r):
            return pltpu.make_async_copy(z_scr, xb_ref.at[pl.ds(r, 1)], zsem)

        for e in range(N_EXPERTS):
            lax.fori_loop(lo_ref[e], hi_ref[e], lambda r, carry: (pad_copy(r).start(), carry)[1], 0)
        for e in range(N_EXPERTS):
            lax.fori_loop(lo_ref[e], hi_ref[e], lambda r, carry: (pad_copy(r).wait(), carry)[1], 0)

    for k in range(TOP_K):
        pltpu.make_async_copy(h_ref, xb_ref.at[pl.ds(0, tm)], sem).wait()


def _dispatch(h2, dest, pad_lo, pad_hi, p_rows):
    m, d = h2.shape
    tm = 1024 if m % 1024 == 0 else 256
    nb = m // tm
    return pl.pallas_call(
        _dispatch_kernel,
        grid_spec=pltpu.PrefetchScalarGridSpec(
            num_scalar_prefetch=2,
            grid=(nb,),
            in_specs=[pl.BlockSpec((1, 1, tm * TOP_K), lambda i, lo, hi: (i, 0, 0), memory_space=pltpu.SMEM),
                      pl.BlockSpec((tm, d), lambda i, lo, hi: (i, 0))],
            out_specs=pl.BlockSpec(memory_space=pl.ANY),
            scratch_shapes=[pltpu.SemaphoreType.DMA, pltpu.SemaphoreType.DMA, pltpu.VMEM((1, d), F32)]),
        out_shape=jax.ShapeDtypeStruct((p_rows, d), F32),
        compiler_params=_cparams("arbitrary"),
        name="moe_dispatch",
    )(pad_lo, pad_hi, dest.reshape(nb, 1, tm * TOP_K), h2)


def _expert_kernel(be_ref, na_ref, x_ref, wgu_ref, bgu_ref, wdn_ref, bdn_ref, y_ref, wgu_scr, wdn_scr):
    i = pl.program_id(0)
    f = wdn_ref.shape[1]
    rows = 128

    @pl.when((i < na_ref[0]) & ((i == 0) | (be_ref[i] != be_ref[jnp.maximum(i - 1, 0)])))
    def _():
        def cast_gu(r, carry):
            sl = pl.ds(pl.multiple_of(r * rows, rows), rows)
            wgu_scr[sl, :] = wgu_ref[0, sl, :].astype(BF16)
            return carry

        def cast_dn(r, carry):
            sl = pl.ds(pl.multiple_of(r * rows, rows), rows)
            wdn_scr[sl, :] = wdn_ref[0, sl, :].astype(BF16)
            return carry

        lax.fori_loop(0, wgu_scr.shape[0] // rows, cast_gu, 0)
        lax.fori_loop(0, wdn_scr.shape[0] // rows, cast_dn, 0)

    @pl.when(i < na_ref[0])
    def _():
        gu = _dot(x_ref[...].astype(BF16), wgu_scr[...]) + bgu_ref[0]
        gate = jnp.minimum(gu[:, :f], SWIGLU_LIMIT)
        up = jnp.clip(gu[:, f:], -SWIGLU_LIMIT, SWIGLU_LIMIT)
        act = (up + 1.0) * gate * _sigmoid(SWIGLU_ALPHA * gate)
        y_ref[...] = _dot(act.astype(BF16), wdn_scr[...]) + bdn_ref[0]

    @pl.when(i >= na_ref[0])
    def _():
        y_ref[...] = jnp.zeros_like(y_ref)


def _experts(xb, block_e, n_active, wgu, bgu, wdn, bdn):
    p_rows, d = xb.shape
    e, _, f2 = wgu.shape
    f = f2 // 2
    blk = MOE_BLOCK
    nb = p_rows // blk
    return pl.pallas_call(
        _expert_kernel,
        grid_spec=pltpu.PrefetchScalarGridSpec(
            num_scalar_prefetch=2,
            grid=(nb,),
            in_specs=[pl.BlockSpec((blk, d), lambda i, be, na: (jnp.minimum(i, na[0] - 1), 0)),
                      pl.BlockSpec((1, d, f2), lambda i, be, na: (be[i], 0, 0)),
                      pl.BlockSpec((1, 1, f2), lambda i, be, na: (be[i], 0, 0)),
                      pl.BlockSpec((1, f, d), lambda i, be, na: (be[i], 0, 0)),
                      pl.BlockSpec((1, 1, d), lambda i, be, na: (be[i], 0, 0))],
            out_specs=pl.BlockSpec((blk, d), lambda i, be, na: (i, 0)),
            scratch_shapes=[pltpu.VMEM((d, f2), BF16), pltpu.VMEM((f, d), BF16)]),
        out_shape=jax.ShapeDtypeStruct((p_rows, d), F32),
        compiler_params=_cparams("arbitrary"),
        name="moe_experts",
    )(block_e, n_active, xb, wgu, bgu.reshape(e, 1, f2), wdn, bdn.reshape(e, 1, d))


def _combine_kernel(dest_ref, dest_next_ref, yb_ref, tw_ref, x1_ref, g2_ref, nw_ref, o_ref, buf, sems):
    i = pl.program_id(0)
    nb = pl.num_programs(0)
    tm = x1_ref.shape[0]
    slot = lax.rem(i, 2)

    def gather_tile(idx_ref, s):
        def issue(r, carry):
            for k in range(TOP_K):
                src = idx_ref[0, 0, r * TOP_K + k]
                pltpu.make_async_copy(yb_ref.at[pl.ds(src, 1)], buf.at[s, k, pl.ds(r, 1)],
                                      sems.at[s]).start(priority=k % 2)
            return carry

        lax.fori_loop(0, tm, issue, 0, unroll=8)

    @pl.when(i == 0)
    def _():
        gather_tile(dest_ref, 0)

    @pl.when(i + 1 < nb)
    def _():
        gather_tile(dest_next_ref, 1 - slot)

    for k in range(TOP_K):
        pltpu.make_async_copy(yb_ref.at[pl.ds(0, tm)], buf.at[slot, k], sems.at[slot]).wait()
    tw = tw_ref[...]
    y = buf[slot, 0] * tw[:, 0:1]
    for k in range(1, TOP_K):
        y = y + buf[slot, k] * tw[:, k:k + 1]
    o_ref[...] = x1_ref[...] + g2_ref[0] * _rms(y, nw_ref[...])


def _combine(yb, dest, tw, x1, g2, nw, t):
    m, d = x1.shape
    tm = 256
    per = t // tm
    nb = m // tm
    return pl.pallas_call(
        _combine_kernel,
        grid=(nb,),
        in_specs=[pl.BlockSpec((1, 1, tm * TOP_K), lambda i: (i, 0, 0), memory_space=pltpu.SMEM),
                  pl.BlockSpec((1, 1, tm * TOP_K), lambda i: (jnp.minimum(i + 1, nb - 1), 0, 0),
                               memory_space=pltpu.SMEM),
                  pl.BlockSpec(memory_space=pl.ANY),
                  pl.BlockSpec((tm, LANES), lambda i: (i, 0)),
                  pl.BlockSpec((tm, d), lambda i: (i, 0)),
                  pl.BlockSpec((1, 1, d), lambda i: (i // per, 0, 0)),
                  pl.BlockSpec((1, d), lambda i: (0, 0))],
        out_specs=pl.BlockSpec((tm, d), lambda i: (i, 0)),
        out_shape=jax.ShapeDtypeStruct((m, d), F32),
        scratch_shapes=[pltpu.VMEM((2, TOP_K, tm, d), F32), pltpu.SemaphoreType.DMA((2,))],
        compiler_params=_cparams("arbitrary"),
        name="moe_combine",
    )(dest.reshape(nb, 1, tm * TOP_K), dest.reshape(nb, 1, tm * TOP_K), yb, tw, x1, g2, nw)


def _gate_rows(p32, batch, t, L):
    g = p32[:, 2 * W_MIX:2 * W_MIX + 4 * N_HEADS].reshape(batch, t // L, L, 4 * N_HEADS)
    return jnp.swapaxes(g, 2, 3)


def kernel(x, c, ctx, c_ctx, w_ada, b_ada, norm_mix_pre, norm_mix_post, norm_ffn_pre, norm_ffn_post, w_in, b_in, conv_w, lb_raw, m_norm, h_norm, w_pa, w_pb, w_out, w_router, b_router, w_gu, b_gu, w_dn, b_dn):
    assert w_in.shape[0] == 1, "single-layer kernel: context tokens only hand over scan states"
    batch, t, d = x.shape
    tc = ctx.shape[1]
    w = W_MIX
    n_tok = batch * t

    rows = -(-(batch + 1) // 8) * 8
    c_all = jnp.zeros((rows, d), F32).at[:batch].set(c).at[batch].set(c_ctx)
    mod = _ada_mod(c_all, w_ada[0], b_ada[0])
    sh1, sc1, g1, sh2, sc2, g2 = [mod[:, i * d:(i + 1) * d] for i in range(6)]
    per_b = lambda a: a[:batch].reshape(batch, 1, d)
    per_c = lambda a: a[batch:batch + 1].reshape(1, 1, d)

    wi, bi = w_in[0], b_in[0]
    hf0, hf1 = 2 * w + 5 * w, 2 * w + 7 * w
    g0 = hf1 + 2 * d
    w16 = jnp.concatenate([wi[:, :2 * w], wi[:, hf1:g0], wi[:, 2 * w:hf0]], axis=1).astype(BF16)
    b16 = jnp.concatenate([bi[:2 * w], bi[hf1:g0], bi[2 * w:hf0]]).reshape(1, -1)
    n32 = 2 * w + LANES
    w32 = jnp.zeros((d, n32), F32).at[:, :2 * w].set(wi[:, hf0:hf1]).at[:, 2 * w:2 * w + 4 * N_HEADS].set(wi[:, g0:]).astype(BF16)
    b32 = jnp.zeros((n32,), F32).at[:2 * w].set(bi[hf0:hf1]).at[2 * w:2 * w + 4 * N_HEADS].set(bi[g0:]).reshape(1, -1)

    x2d = x.reshape(n_tok, d)
    c2d = ctx.reshape(batch * tc, d)
    nmp = norm_mix_pre[0].reshape(1, d)
    n16 = w16.shape[1]
    tn16 = n16 // 2
    p16_l = _proj(x2d, nmp, per_b(sc1), per_b(sh1), w16, b16, t, BF16, tn16)
    p32_l = _proj(x2d, nmp, per_b(sc1), per_b(sh1), w32, b32, t, F32, n32)
    p16_c = _proj(c2d, nmp, per_c(sc1), per_c(sh1), w16, b16, batch * tc, BF16, tn16)
    p32_c = _proj(c2d, nmp, per_c(sc1), per_c(sh1), w32, b32, batch * tc, F32, n32)

    cw9 = jnp.zeros((16, 2 * w), F32).at[:9].set(conv_w[0].reshape(9, 2 * w))
    qk_l = _conv_silu(p16_l, cw9, batch, t, GRID_W)
    qk_c = _conv_silu(p16_c, cw9, batch, tc, tc)

    a = _mlstm(qk_l, p16_l, p32_l, _gate_rows(p32_l, batch, t, MLSTM_CHUNK),
               qk_c, p16_c, p32_c, _gate_rows(p32_c, batch, tc, MLSTM_CHUNK),
               m_norm[0].reshape(1, w), batch, t, tc)

    lb_all = jnp.cumsum(jax.nn.softmax(lb_raw.astype(F32), axis=0), axis=0)
    b = _hgrn(p16_l, p32_l, p16_c, p32_c, lb_all[0], h_norm[0].reshape(1, w), batch, t, tc)

    wr = jnp.zeros((d, LANES), F32).at[:, :N_EXPERTS].set(w_router[0])
    br = jnp.full((1, LANES), NEG_BIG, F32).at[0, :N_EXPERTS].set(b_router[0])
    x1, h2, te, tw, rk, cnt = _merge_route(
        a, b, p16_l, x2d, per_b(g1), per_b(sh2), per_b(sc2),
        norm_mix_post[0].reshape(1, d), norm_ffn_pre[0].reshape(1, d),
        w_pa[0].astype(BF16), w_pb[0].astype(BF16), w_out[0].astype(BF16), wr, br, t)

    blk = MOE_BLOCK
    counts = cnt[0, :N_EXPERTS].astype(I32)
    pcounts = (counts + blk - 1) // blk * blk
    pend = jnp.cumsum(pcounts)
    pstart = pend - pcounts
    dest = (pstart[te[:, :TOP_K]] + rk[:, :TOP_K]).reshape(-1)
    p_rows = -(-n_tok * TOP_K // blk) * blk + N_EXPERTS * blk
    n_blocks = p_rows // blk
    block_row0 = jnp.arange(n_blocks, dtype=I32) * blk
    block_e = jnp.minimum(jnp.sum((pend[None, :] <= block_row0[:, None]).astype(I32), axis=1), N_EXPERTS - 1)
    n_active = (pend[-1:] // blk).astype(I32)

    xb = _dispatch(h2, dest, pstart + counts, pend, p_rows)
    yb = _experts(xb, block_e, n_active, w_gu[0], b_gu[0], w_dn[0], b_dn[0])
    out = _combine(yb, dest, tw, x1, per_b(g2), norm_ffn_post[0].reshape(1, d), t)
    return out.reshape(batch, t, d)
```
